```python
import jax, jax.numpy as jnp
from jax import lax
import numpy as np

D_MODEL = 2048
BATCH = 2
SEQ = 16384
DEPTH = 1

HEAD_DIM = 128
N_MIX_HEADS = D_MODEL // HEAD_DIM
GMLP_GROUPS = N_MIX_HEADS // 4
NSA_HEADS = N_MIX_HEADS // 2
NSA_KV_HEADS = 2
XATTN_HEADS = N_MIX_HEADS // 4
GMLP_WIDTH = GMLP_GROUPS * HEAD_DIM
NSA_WIDTH = NSA_HEADS * HEAD_DIM
NSA_KV_WIDTH = NSA_KV_HEADS * HEAD_DIM
XATTN_WIDTH = XATTN_HEADS * HEAD_DIM
MIX_WIDTH = GMLP_WIDTH + NSA_WIDTH + XATTN_WIDTH
MEM_TOKENS = 256

GMLP_CHUNK = 128
CMP_LEN = 32
CMP_STRIDE = 16
SEL_BLOCK = 64
SEL_TOPK = 16
WINDOW = 512
Q_BLOCK = 128

N_EXPERTS = 32
TOP_K = 4
D_FF_EXPERT = D_MODEL
SWIGLU_ALPHA = 1.702
SWIGLU_LIMIT = 7.0
MOE_BLOCK = 512

ROPE_THETA = 10000.0
EPS = 1e-6

COL_SIZES = (2 * GMLP_WIDTH,
             NSA_WIDTH,
             6 * NSA_KV_WIDTH,
             3 * NSA_HEADS,
             XATTN_WIDTH)
IN_COLS = sum(COL_SIZES)

kernel_name = "hybrid_gmlp_nsa_xattn_moe_layer"


def rms_norm(x, g):
    xf = x.astype(jnp.float32)
    y = xf * lax.rsqrt(jnp.mean(xf * xf, axis=-1, keepdims=True) + EPS)
    return (y * g.astype(jnp.float32)).astype(x.dtype)


def rope(x, pos):
    dh = x.shape[-1]
    half = dh // 2
    inv = ROPE_THETA ** (-jnp.arange(half, dtype=jnp.float32) * 2.0 / dh)
    ang = pos.astype(jnp.float32)[:, :, None, None] * inv
    cos, sin = jnp.cos(ang), jnp.sin(ang)
    x1 = x[..., :half].astype(jnp.float32)
    x2 = x[..., half:].astype(jnp.float32)
    return jnp.concatenate([x1 * cos - x2 * sin, x2 * cos + x1 * sin], axis=-1).astype(x.dtype)


def masked_softmax(s, mask):
    s = jnp.where(mask, s.astype(jnp.float32), -jnp.inf)
    m = jnp.max(s, axis=-1, keepdims=True)
    m = jnp.where(jnp.isfinite(m), m, 0.0)
    e = jnp.exp(s - m)
    d = jnp.sum(e, axis=-1, keepdims=True)
    return e / jnp.where(d > 0, d, 1.0)


def gmlp_mixer(uv, w_s, b_s, g_v):
    B, S, _ = uv.shape
    uv = jax.nn.gelu(uv)
    u, v = jnp.split(uv, 2, axis=-1)
    v = rms_norm(v.reshape(B, S, GMLP_GROUPS, HEAD_DIM), g_v)
    v = v.reshape(B, S // GMLP_CHUNK, GMLP_CHUNK, GMLP_GROUPS, HEAD_DIM)
    causal = jnp.tril(jnp.ones((GMLP_CHUNK, GMLP_CHUNK), w_s.dtype))
    w = w_s * causal[None]
    vs = jnp.einsum('gts,bnsgd->bntgd', w, v) + jnp.transpose(b_s)[None, None, :, :, None]
    return u * vs.reshape(B, S, GMLP_WIDTH)


def nsa_mixer(q, k_cmp, v_cmp, k_sel, v_sel, k_win, v_win, gate_logits, positions,
              cmp_pos, w_ck1, w_ck2, w_cv1, w_cv2, q_g, k_g):
    B, S, H, dh = q.shape
    G = k_cmp.shape[2]
    R = H // G
    scale = dh ** -0.5
    n_cmp = (S - CMP_LEN) // CMP_STRIDE + 1
    n_sel = S // SEL_BLOCK
    n_qb = S // Q_BLOCK
    k_top = min(SEL_TOPK, n_sel)

    cmp_idx = jnp.arange(n_cmp)[:, None] * CMP_STRIDE + jnp.arange(CMP_LEN)[None, :]

    def compress(t, w1, w2):
        blk = t[:, cmp_idx] + cmp_pos[None, None, :, None, :].astype(t.dtype)
        blk = jnp.transpose(blk, (0, 1, 3, 2, 4)).reshape(B, n_cmp, G, CMP_LEN * dh)
        return jax.nn.gelu(blk @ w1) @ w2

    kc = rms_norm(compress(k_cmp, w_ck1, w_ck2), k_g)
    vc = compress(v_cmp, w_cv1, w_cv2)
    cmp_end = jnp.arange(n_cmp) * CMP_STRIDE + CMP_LEN - 1

    qn = rms_norm(q, q_g)
    qr = rope(qn, positions)
    ks = rope(rms_norm(k_sel, k_g), positions)
    kw = rope(rms_norm(k_win, k_g), positions)

    ks_blk = jnp.transpose(ks.reshape(B, n_sel, SEL_BLOCK, G, dh), (0, 3, 1, 2, 4))
    vs_blk = jnp.transpose(v_sel.reshape(B, n_sel, SEL_BLOCK, G, dh), (0, 3, 1, 2, 4))
    pad = ((0, 0), (WINDOW, 0), (0, 0), (0, 0))
    kw_pad = jnp.pad(kw, pad)
    vw_pad = jnp.pad(v_win, pad)

    ratio = SEL_BLOCK // CMP_STRIDE
    offs = (jnp.arange(ratio)[:, None] - jnp.arange(CMP_LEN // CMP_STRIDE)[None, :]).reshape(-1)
    diff = jnp.arange(n_cmp)[:, None] - ratio * jnp.arange(n_sel)[None, :]
    cmp_to_sel = (diff[..., None] == offs).sum(-1).astype(jnp.float32)
    sel_start = jnp.arange(n_sel) * SEL_BLOCK
    sel_ids = jnp.arange(n_sel)

    def to_blocks(a):
        a = a.reshape((B, n_qb, Q_BLOCK) + a.shape[2:])
        return jnp.moveaxis(a, 1, 0)

    qn_b = to_blocks(qn.reshape(B, S, G, R, dh))
    qr_b = to_blocks(qr.reshape(B, S, G, R, dh))
    gt_b = to_blocks(jax.nn.sigmoid(gate_logits.astype(jnp.float32)).reshape(B, S, G, R, 3))

    gather2 = jax.vmap(jax.vmap(lambda kb, ix: kb[ix]))

    def block(args):
        bi, qn_i, qr_i, g_i = args
        s0 = bi * Q_BLOCK
        t = s0 + jnp.arange(Q_BLOCK)
        sc = jnp.einsum('bqgrd,bngd->bgrqn', qn_i, kc) * scale
        pc = masked_softmax(sc, cmp_end[None, :] <= t[:, None])
        o_c = jnp.einsum('bgrqn,bngd->bqgrd', pc.astype(vc.dtype), vc)
        imp = jnp.einsum('bgqn,ns->bgqs', pc.sum(axis=2), cmp_to_sel)
        cur = t // SEL_BLOCK
        valid = sel_start[None, :] <= t[:, None]
        forced = (sel_ids[None, :] == 0) | (sel_ids[None, :] == cur[:, None]) | (sel_ids[None, :] == cur[:, None] - 1)
        score = jnp.where(valid, jnp.where(forced, jnp.inf, imp), -jnp.inf)
        _, idx = lax.top_k(score, k_top)
        kg = gather2(ks_blk, idx)
        vg = gather2(vs_blk, idx)
        ss = jnp.einsum('bqgrd,bgqksd->bgrqks', qr_i, kg) * scale
        tok = idx[..., None] * SEL_BLOCK + jnp.arange(SEL_BLOCK)
        smask = (tok <= t[None, None, :, None, None]).reshape(B, G, 1, Q_BLOCK, -1)
        ps = masked_softmax(ss.reshape(B, G, R, Q_BLOCK, -1), smask).reshape(ss.shape)
        o_s = jnp.einsum('bgrqks,bgqksd->bqgrd', ps.astype(vg.dtype), vg)
        kwi = lax.dynamic_slice_in_dim(kw_pad, s0, Q_BLOCK + WINDOW, axis=1)
        vwi = lax.dynamic_slice_in_dim(vw_pad, s0, Q_BLOCK + WINDOW, axis=1)
        kp = s0 - WINDOW + jnp.arange(Q_BLOCK + WINDOW)
        wmask = (kp[None, :] <= t[:, None]) & (kp[None, :] > t[:, None] - WINDOW) & (kp[None, :] >= 0)
        sw = jnp.einsum('bqgrd,bkgd->bgrqk', qr_i, kwi) * scale
        pw = masked_softmax(sw, wmask)
        o_w = jnp.einsum('bgrqk,bkgd->bqgrd', pw.astype(vwi.dtype), vwi)
        g_i = g_i.astype(o_c.dtype)
        return g_i[..., 0:1] * o_c + g_i[..., 1:2] * o_s + g_i[..., 2:3] * o_w

    out = lax.map(block, (jnp.arange(n_qb), qn_b, qr_b, gt_b))
    return jnp.moveaxis(out, 0, 1).reshape(B, S, H * dh)


def memory_cross_attn(q, mem_kv, q_g, k_g):
    B, M, _ = mem_kv.shape
    k, v = jnp.split(mem_kv, 2, axis=-1)
    k = rms_norm(k.reshape(B, M, XATTN_HEADS, HEAD_DIM), k_g)
    v = v.reshape(B, M, XATTN_HEADS, HEAD_DIM)
    q = rms_norm(q, q_g)
    s = jnp.einsum('bshd,bmhd->bhsm', q, k) * (HEAD_DIM ** -0.5)
    p = jax.nn.softmax(s.astype(jnp.float32), axis=-1)
    o = jnp.einsum('bhsm,bmhd->bshd', p.astype(v.dtype), v)
    return o.reshape(q.shape[0], q.shape[1], XATTN_WIDTH)


def moe_ffn(h, w_router, b_router, w1, b1, w2, b2):
    B, S, D = h.shape
    T = B * S
    xs = h.reshape(T, D)
    logits = (xs @ w_router + b_router).astype(jnp.float32)
    top_vals, top_idx = lax.top_k(logits, TOP_K)
    gate = jax.nn.softmax(top_vals, axis=-1).astype(xs.dtype)
    N = T * TOP_K
    e_flat = top_idx.reshape(N)
    tok_flat = jnp.repeat(jnp.arange(T, dtype=jnp.int32), TOP_K)
    g_flat = gate.reshape(N)
    order = jnp.argsort(e_flat)
    e_sorted = e_flat[order]
    tok_sorted = tok_flat[order]
    g_sorted = g_flat[order]
    counts = jnp.bincount(e_flat, length=N_EXPERTS)
    start = jnp.cumsum(counts) - counts
    padded = ((counts + MOE_BLOCK - 1) // MOE_BLOCK) * MOE_BLOCK
    pend = jnp.cumsum(padded)
    pstart = pend - padded
    dest = pstart[e_sorted] + (jnp.arange(N) - start[e_sorted])
    n_blocks = -(-(N + N_EXPERTS * (MOE_BLOCK - 1)) // MOE_BLOCK)
    rows = n_blocks * MOE_BLOCK
    row_tok = jnp.zeros((rows,), jnp.int32).at[dest].set(tok_sorted)
    row_g = jnp.zeros((rows,), xs.dtype).at[dest].set(g_sorted)
    block_e = jnp.minimum(jnp.searchsorted(pend, jnp.arange(n_blocks) * MOE_BLOCK, side='right'), N_EXPERTS - 1)

    def body(acc, inp):
        e, tok, g = inp
        hb = xs[tok] @ w1[e] + b1[e]
        glu = jnp.minimum(hb[:, 0::2], SWIGLU_LIMIT)
        lin = jnp.clip(hb[:, 1::2], -SWIGLU_LIMIT, SWIGLU_LIMIT)
        act = glu * jax.nn.sigmoid(SWIGLU_ALPHA * glu) * (lin + 1)
        yb = act @ w2[e] + b2[e]
        return acc.at[tok].add(yb * g[:, None]), None

    acc, _ = lax.scan(body, jnp.zeros((T, D), xs.dtype),
                      (block_e, row_tok.reshape(n_blocks, MOE_BLOCK), row_g.reshape(n_blocks, MOE_BLOCK)))
    return acc.reshape(B, S, D)


def setup_inputs(seed: int = 0) -> dict:
    key = jax.random.key(seed)
    ks = jax.random.split(key, 32)
    f32 = jnp.float32
    L = DEPTH

    def nrm(k, shape, fan_in):
        return jax.random.normal(k, shape, f32) * (fan_in ** -0.5)

    def gain(k, shape):
        return 1.0 + 0.02 * jax.random.normal(k, shape, f32)

    return {
        "x": jax.random.normal(ks[0], (BATCH, SEQ, D_MODEL), f32),
        "mem": jax.random.normal(ks[1], (BATCH, MEM_TOKENS, D_MODEL), f32),
        "positions": jnp.broadcast_to(jnp.arange(SEQ, dtype=jnp.int32), (BATCH, SEQ)),
        "mix_norm_g": gain(ks[2], (L, D_MODEL)),
        "mem_norm_g": gain(ks[3], (L, D_MODEL)),
        "w_in": nrm(ks[4], (L, D_MODEL, IN_COLS), D_MODEL),
        "w_out": nrm(ks[5], (L, MIX_WIDTH, D_MODEL), MIX_WIDTH),
        "gmlp_ws": nrm(ks[6], (L, GMLP_GROUPS, GMLP_CHUNK, GMLP_CHUNK), GMLP_CHUNK),
        "gmlp_bs": gain(ks[7], (L, GMLP_GROUPS, GMLP_CHUNK)),
        "gmlp_vg": gain(ks[8], (L, GMLP_GROUPS, HEAD_DIM)),
        "nsa_cmp_pos": 0.5 * jax.random.normal(ks[9], (L, CMP_LEN, HEAD_DIM), f32),
        "nsa_ck1": nrm(ks[10], (L, CMP_LEN * HEAD_DIM, HEAD_DIM), CMP_LEN * HEAD_DIM),
        "nsa_ck2": nrm(ks[11], (L, HEAD_DIM, HEAD_DIM), HEAD_DIM),
        "nsa_cv1": nrm(ks[12], (L, CMP_LEN * HEAD_DIM, HEAD_DIM), CMP_LEN * HEAD_DIM),
        "nsa_cv2": nrm(ks[13], (L, HEAD_DIM, HEAD_DIM), HEAD_DIM),
        "nsa_q_g": gain(ks[14], (L, HEAD_DIM)),
        "nsa_k_g": gain(ks[15], (L, HEAD_DIM)),
        "xattn_wkv": nrm(ks[16], (L, D_MODEL, 2 * XATTN_WIDTH), D_MODEL),
        "xattn_q_g": gain(ks[17], (L, HEAD_DIM)),
        "xattn_k_g": gain(ks[18], (L, HEAD_DIM)),
        "ffn_norm_g": gain(ks[19], (L, D_MODEL)),
        "w_router": nrm(ks[20], (L, D_MODEL, N_EXPERTS), D_MODEL),
        "b_router": 0.01 * jax.random.normal(ks[21], (L, N_EXPERTS), f32),
        "w_e1": nrm(ks[22], (L, N_EXPERTS, D_MODEL, 2 * D_FF_EXPERT), D_MODEL),
        "b_e1": 0.01 * jax.random.normal(ks[23], (L, N_EXPERTS, 2 * D_FF_EXPERT), f32),
        "w_e2": nrm(ks[24], (L, N_EXPERTS, D_FF_EXPERT, D_MODEL), D_FF_EXPERT),
        "b_e2": 0.01 * jax.random.normal(ks[25], (L, N_EXPERTS, D_MODEL), f32),
    }


def reference(x, mem, positions, mix_norm_g, mem_norm_g, w_in, w_out, gmlp_ws, gmlp_bs, gmlp_vg,
              nsa_cmp_pos, nsa_ck1, nsa_ck2, nsa_cv1, nsa_cv2, nsa_q_g, nsa_k_g,
              xattn_wkv, xattn_q_g, xattn_k_g, ffn_norm_g, w_router, b_router,
              w_e1, b_e1, w_e2, b_e2):
    B, S, _ = x.shape
    split_pts = [int(v) for v in np.cumsum(COL_SIZES)[:-1]]
    for l in range(DEPTH):
        h = rms_norm(x, mix_norm_g[l])
        proj = h @ w_in[l]
        uv, q_nsa, kv_nsa, gate_nsa, q_x = jnp.split(proj, split_pts, axis=-1)
        y_a = gmlp_mixer(uv, gmlp_ws[l], gmlp_bs[l], gmlp_vg[l])
        kvs = [t.reshape(B, S, NSA_KV_HEADS, HEAD_DIM) for t in jnp.split(kv_nsa, 6, axis=-1)]
        y_b = nsa_mixer(q_nsa.reshape(B, S, NSA_HEADS, HEAD_DIM), kvs[0], kvs[1], kvs[2], kvs[3],
                        kvs[4], kvs[5], gate_nsa, positions, nsa_cmp_pos[l], nsa_ck1[l], nsa_ck2[l],
                        nsa_cv1[l], nsa_cv2[l], nsa_q_g[l], nsa_k_g[l])
        mem_kv = rms_norm(mem, mem_norm_g[l]) @ xattn_wkv[l]
        y_c = memory_cross_attn(q_x.reshape(B, S, XATTN_HEADS, HEAD_DIM), mem_kv, xattn_q_g[l], xattn_k_g[l])
        mix = jnp.concatenate([y_a, y_b, y_c], axis=-1)
        x = x + mix @ w_out[l]
        h2 = rms_norm(x, ffn_norm_g[l])
        x = x + moe_ffn(h2, w_router[l], b_router[l], w_e1[l], b_e1[l], w_e2[l], b_e2[l])
    return x
```

```python
import functools

import numpy as np
import jax
import jax.numpy as jnp
from jax import lax
from jax.experimental import pallas as pl
from jax.experimental.pallas import tpu as pltpu

F32 = jnp.float32
BF16 = jnp.bfloat16

HEAD_DIM = 128
GMLP_GROUPS = 4
NSA_HEADS = 8
NSA_KV_HEADS = 2
NSA_REP = NSA_HEADS // NSA_KV_HEADS
XATTN_HEADS = 4
GMLP_WIDTH = GMLP_GROUPS * HEAD_DIM
NSA_WIDTH = NSA_HEADS * HEAD_DIM
XATTN_WIDTH = XATTN_HEADS * HEAD_DIM
GMLP_CHUNK = 128
CMP_LEN = 32
CMP_STRIDE = 16
SEL_BLOCK = 64
SEL_TOPK = 16
WINDOW = 512
Q_BLOCK = 128
TOP_K = 4
SWIGLU_ALPHA = 1.702
SWIGLU_LIMIT = 7.0
MOE_BLOCK = 512
ROPE_THETA = 10000.0
EPS = 1e-6
NEG_BIG = -1e30

COL_UV = 0
COL_Q = 8
COL_KV = 16
COL_QX = 28
COL_GATE = 32
N_COL_BLOCKS = 34
PROJ_COLS = N_COL_BLOCKS * 128

VMEM_LIMIT = 56 * 1024 * 1024


def _cparams(n_axes):
    return pltpu.CompilerParams(
        dimension_semantics=("arbitrary",) * n_axes, vmem_limit_bytes=VMEM_LIMIT)


def _rms(x, g):
    ms = jnp.mean(x * x, axis=-1, keepdims=True)
    return x * lax.rsqrt(ms + EPS) * g


def _gelu_tanh(x):
    c = np.float32(np.sqrt(2.0 / np.pi))
    return 0.5 * x * (1.0 + jnp.tanh(c * (x + 0.044715 * (x * x * x))))


def _sigmoid(x):
    return 1.0 / (1.0 + jnp.exp(-x))


def _dot(a, b):
    return jnp.dot(a, b, preferred_element_type=F32)


def _dot_nt(a, b):
    return lax.dot_general(a, b, (((1,), (1,)), ((), ())), preferred_element_type=F32)


def _inproj_kernel(x_ref, g_ref, w_ref, o_ref, h_ref):
    @pl.when(pl.program_id(1) == 0)
    def _():
        h_ref[...] = _rms(x_ref[...], g_ref[...]).astype(BF16)

    o_ref[...] = _dot(h_ref[...], w_ref[...])


def _in_proj(x2, g, w_r, tm, tn):
    T, D = x2.shape
    NC = w_r.shape[1]
    return pl.pallas_call(
        _inproj_kernel,
        grid=(T // tm, NC // tn),
        in_specs=[
            pl.BlockSpec((tm, D), lambda i, j: (i, 0)),
            pl.BlockSpec((1, D), lambda i, j: (0, 0)),
            pl.BlockSpec((D, tn), lambda i, j: (0, j)),
        ],
        out_specs=pl.BlockSpec((tm, tn), lambda i, j: (i, j)),
        out_shape=jax.ShapeDtypeStruct((T, NC), F32),
        scratch_shapes=[pltpu.VMEM((tm, D), BF16)],
        compiler_params=_cparams(2),
    )(x2, g, w_r)


def _prep_kernel(q_ref, sel_ref, win_ref, ang_ref, qg_ref, kg_ref,
                 qn_o, qr_o, ks_o, vs_o, kw_o, vw_o):
    ang = ang_ref[...]
    lane = lax.broadcasted_iota(jnp.int32, (1, HEAD_DIM), 1)
    cosf = jnp.cos(ang)
    sinf = jnp.sin(ang) * jnp.where(lane < HEAD_DIM // 2, -1.0, 1.0)
    scale = np.float32(HEAD_DIM ** -0.5)

    def rope(x):
        return x * cosf + pltpu.roll(x, HEAD_DIM // 2, axis=1) * sinf

    qg = qg_ref[...]
    kg = kg_ref[...]
    for h in range(NSA_HEADS):
        sl = slice(h * HEAD_DIM, (h + 1) * HEAD_DIM)
        n = _rms(q_ref[:, sl], qg)
        qn_o[:, sl] = (n * scale).astype(BF16)
        qr_o[:, sl] = (rope(n) * scale).astype(BF16)
    for g in range(NSA_KV_HEADS):
        sl = slice(g * HEAD_DIM, (g + 1) * HEAD_DIM)
        sv = slice((NSA_KV_HEADS + g) * HEAD_DIM, (NSA_KV_HEADS + g + 1) * HEAD_DIM)
        ks_o[g] = rope(_rms(sel_ref[:, sl], kg)).astype(BF16)
        vs_o[g] = sel_ref[:, sv].astype(BF16)
        kw_o[g] = rope(_rms(win_ref[:, sl], kg)).astype(BF16)
        vw_o[g] = win_ref[:, sv].astype(BF16)


def _prep(proj, ang, q_g, k_g, B, S, tm):
    T = B * S
    nsb = S // tm
    kv_shape = jax.ShapeDtypeStruct((B, NSA_KV_HEADS, S, HEAD_DIM), BF16)
    kv_spec = pl.BlockSpec((None, NSA_KV_HEADS, tm, HEAD_DIM),
                           lambda i: (i // nsb, 0, i % nsb, 0))
    return pl.pallas_call(
        _prep_kernel,
        grid=(T // tm,),
        in_specs=[
            pl.BlockSpec((tm, NSA_WIDTH), lambda i: (i, COL_Q * 128 // NSA_WIDTH)),
            pl.BlockSpec((tm, 512), lambda i: (i, (COL_KV + 4) * 128 // 512)),
            pl.BlockSpec((tm, 512), lambda i: (i, (COL_KV + 8) * 128 // 512)),
            pl.BlockSpec((tm, HEAD_DIM), lambda i: (i, 0)),
            pl.BlockSpec((1, HEAD_DIM), lambda i: (0, 0)),
            pl.BlockSpec((1, HEAD_DIM), lambda i: (0, 0)),
        ],
        out_specs=[
            pl.BlockSpec((tm, NSA_WIDTH), lambda i: (i, 0)),
            pl.BlockSpec((tm, NSA_WIDTH), lambda i: (i, 0)),
            kv_spec, kv_spec, kv_spec, kv_spec,
        ],
        out_shape=[
            jax.ShapeDtypeStruct((T, NSA_WIDTH), BF16),
            jax.ShapeDtypeStruct((T, NSA_WIDTH), BF16),
            kv_shape, kv_shape, kv_shape, kv_shape,
        ],
        compiler_params=_cparams(1),
    )(proj, proj, proj, ang, q_g, k_g)


def _gmlp_kernel(uv_ref, ws_ref, bias_ref, gv_ref, o_ref, *, n_chunks):
    row = lax.broadcasted_iota(jnp.int32, (GMLP_CHUNK, GMLP_CHUNK), 0)
    col = lax.broadcasted_iota(jnp.int32, (GMLP_CHUNK, GMLP_CHUNK), 1)
    causal = col <= row
    for g in range(GMLP_GROUPS):
        sl = slice(g * HEAD_DIM, (g + 1) * HEAD_DIM)
        sv = slice(GMLP_WIDTH + g * HEAD_DIM, GMLP_WIDTH + (g + 1) * HEAD_DIM)
        w = jnp.where(causal, ws_ref[g], 0.0).astype(BF16)
        u = _gelu_tanh(uv_ref[:, sl])
        v = _rms(_gelu_tanh(uv_ref[:, sv]), gv_ref[:, sl]).astype(BF16)
        bias = bias_ref[:, sl]
        for c in range(n_chunks):
            rs = slice(c * GMLP_CHUNK, (c + 1) * GMLP_CHUNK)
            vs = _dot(w, v[rs]) + bias
            o_ref[rs, sl] = (u[rs] * vs).astype(BF16)


def _gmlp(proj, ws, bias_full, gv, T, tm):
    return pl.pallas_call(
        functools.partial(_gmlp_kernel, n_chunks=tm // GMLP_CHUNK),
        grid=(T // tm,),
        in_specs=[
            pl.BlockSpec((tm, 2 * GMLP_WIDTH), lambda i: (i, 0)),
            pl.BlockSpec((GMLP_GROUPS, GMLP_CHUNK, GMLP_CHUNK), lambda i: (0, 0, 0)),
            pl.BlockSpec((GMLP_CHUNK, GMLP_WIDTH), lambda i: (0, 0)),
            pl.BlockSpec((1, GMLP_WIDTH), lambda i: (0, 0)),
        ],
        out_specs=pl.BlockSpec((tm, GMLP_WIDTH), lambda i: (i, 0)),
        out_shape=jax.ShapeDtypeStruct((T, GMLP_WIDTH), BF16),
        compiler_params=_cparams(1),
    )(proj, ws, bias_full, gv)


def _compress_kernel(x_ref, w1_ref, w2_ref, pos_ref, kg_ref, o_ref):
    nc = x_ref.shape[0]
    kind = pl.program_id(1)
    acc_a = jnp.zeros((nc, HEAD_DIM), F32)
    acc_b = jnp.zeros((nc, HEAD_DIM), F32)
    for j in range(CMP_STRIDE):
        xj = x_ref[:, j, :]
        wa = w1_ref[j * HEAD_DIM:(j + 1) * HEAD_DIM, :]
        wb = w1_ref[(CMP_STRIDE + j) * HEAD_DIM:(CMP_STRIDE + j + 1) * HEAD_DIM, :]
        acc_a += _dot((xj + pos_ref[j:j + 1, :]).astype(BF16), wa)
        acc_b += _dot((xj + pos_ref[CMP_STRIDE + j:CMP_STRIDE + j + 1, :]).astype(BF16), wb)
    hidden = acc_a + pltpu.roll(acc_b, nc - 1, axis=0)
    out = _dot(_gelu_tanh(hidden).astype(BF16), w2_ref[...])
    o_ref[...] = jnp.where(kind == 0, _rms(out, kg_ref[...]), out).astype(BF16)


def _compress(proj4, w1, w2, cmp_pos, k_g, B, NC):
    return pl.pallas_call(
        _compress_kernel,
        grid=(B, 2, NSA_KV_HEADS),
        in_specs=[
            pl.BlockSpec((None, NC, CMP_STRIDE, HEAD_DIM),
                         lambda b, k, g: (b, 0, 0, COL_KV + 2 * k + g)),
            pl.BlockSpec((None, CMP_LEN * HEAD_DIM, HEAD_DIM), lambda b, k, g: (k, 0, 0)),
            pl.BlockSpec((None, HEAD_DIM, HEAD_DIM), lambda b, k, g: (k, 0, 0)),
            pl.BlockSpec((CMP_LEN, HEAD_DIM), lambda b, k, g: (0, 0)),
            pl.BlockSpec((1, HEAD_DIM), lambda b, k, g: (0, 0)),
        ],
        out_specs=pl.BlockSpec((None, None, None, NC, HEAD_DIM), lambda b, k, g: (b, k, g, 0, 0)),
        out_shape=jax.ShapeDtypeStruct((B, 2, NSA_KV_HEADS, NC, HEAD_DIM), BF16),
        compiler_params=_cparams(3),
    )(proj4, w1, w2, cmp_pos, k_g)


def _cmp_attn_kernel(q_ref, kc_ref, vc_ref, gl_ref, c2s_ref, oc_ref, sel_ref, *, k_top):
    i = pl.program_id(2)
    tq = q_ref.shape[0]
    nc = kc_ref.shape[0]
    n_sel = sel_ref.shape[1]
    t = i * tq + lax.broadcasted_iota(jnp.int32, (tq, 1), 0)
    cmp_end = lax.broadcasted_iota(jnp.int32, (1, nc), 1) * CMP_STRIDE + (CMP_LEN - 1)
    cmask = cmp_end <= t
    kc = kc_ref[...]
    vc = vc_ref[...]
    gl = gl_ref[...]
    pc_sum = jnp.zeros((tq, nc), F32)
    for r in range(NSA_REP):
        sl = slice(r * HEAD_DIM, (r + 1) * HEAD_DIM)
        s = jnp.where(cmask, _dot_nt(q_ref[:, sl], kc), -jnp.inf)
        m = jnp.max(s, axis=-1, keepdims=True)
        m = jnp.where(m > -jnp.inf, m, 0.0)
        e = jnp.exp(s - m)
        d = jnp.sum(e, axis=-1, keepdims=True)
        p = e / jnp.where(d > 0, d, 1.0)
        gate = _sigmoid(gl[:, 3 * r:3 * r + 1])
        oc_ref[:, sl] = _dot(p.astype(BF16), vc) * gate
        pc_sum += p
    hi = pc_sum.astype(BF16)
    lo = (pc_sum - hi.astype(F32)).astype(BF16)
    imp = _dot(hi, c2s_ref[...]) + _dot(lo, c2s_ref[...])

    sid = lax.broadcasted_iota(jnp.int32, (1, n_sel), 1)
    sid_f = sid.astype(F32)
    cur = t // SEL_BLOCK
    valid = sid * SEL_BLOCK <= t
    forced = (sid == 0) | (sid == cur) | (sid == cur - 1)
    score = jnp.where(valid, jnp.where(forced, jnp.inf, imp), -jnp.inf)
    sel = jnp.zeros((tq, n_sel), F32)
    for _ in range(k_top):
        m = jnp.max(score, axis=-1, keepdims=True)
        idx = jnp.min(jnp.where(score == m, sid_f, np.float32(n_sel)), axis=-1, keepdims=True)
        pick = sid_f == idx
        sel = jnp.where(pick, 1.0, sel)
        score = jnp.where(pick, -jnp.inf, score)
    sel_ref[...] = sel.astype(BF16)


def _cmp_attn(qn, cmp_kv, proj, c2s, B, S, tq):
    T = B * S
    nqb = S // tq
    NC = cmp_kv.shape[3]
    n_sel = S // SEL_BLOCK
    gw = NSA_REP * HEAD_DIM
    return pl.pallas_call(
        functools.partial(_cmp_attn_kernel, k_top=min(SEL_TOPK, n_sel)),
        grid=(B, NSA_KV_HEADS, nqb),
        in_specs=[
            pl.BlockSpec((tq, gw), lambda b, g, i: (b * nqb + i, g)),
            pl.BlockSpec((None, None, None, NC, HEAD_DIM), lambda b, g, i: (b, 0, g, 0, 0)),
            pl.BlockSpec((None, None, None, NC, HEAD_DIM), lambda b, g, i: (b, 1, g, 0, 0)),
            pl.BlockSpec((tq, 128), lambda b, g, i: (b * nqb + i, COL_GATE + g)),
            pl.BlockSpec((NC, n_sel), lambda b, g, i: (0, 0)),
        ],
        out_specs=[
            pl.BlockSpec((tq, gw), lambda b, g, i: (b * nqb + i, g)),
            pl.BlockSpec((None, None, tq, n_sel), lambda b, g, i: (b, g, i, 0)),
        ],
        out_shape=[
            jax.ShapeDtypeStruct((T, NSA_WIDTH), F32),
            jax.ShapeDtypeStruct((B, NSA_KV_HEADS, S, n_sel), BF16),
        ],
        compiler_params=_cparams(3),
    )(qn, cmp_kv, cmp_kv, proj, c2s)


def _sel_attn_kernel(q_ref, k_ref, v_ref, gl_ref, sel_ref, eb_ref, o_ref, m_sc, l_sc, acc_sc, *, tk):
    i = pl.program_id(2)
    tq = q_ref.shape[0]
    s0 = i * tq
    t = s0 + lax.broadcasted_iota(jnp.int32, (tq, 1), 0)
    m_sc[...] = jnp.full(m_sc.shape, NEG_BIG, F32)
    l_sc[...] = jnp.zeros(l_sc.shape, F32)
    acc_sc[...] = jnp.zeros(acc_sc.shape, F32)
    key0 = lax.broadcasted_iota(jnp.int32, (1, tk), 1)
    blocks_per_tile = tk // SEL_BLOCK

    def body(j, carry):
        koff = pl.multiple_of(j * tk, tk)
        k = k_ref[pl.ds(koff, tk), :]
        v = v_ref[pl.ds(koff, tk), :]
        expand = jnp.where(eb_ref[...] == j * blocks_per_tile, 1.0, 0.0).astype(BF16)
        sel_tok = _dot(sel_ref[...], expand)
        allowed = (sel_tok > 0.5) & (key0 + koff <= t)
        for r in range(NSA_REP):
            sl = slice(r * HEAD_DIM, (r + 1) * HEAD_DIM)
            s = jnp.where(allowed, _dot_nt(q_ref[:, sl], k), NEG_BIG)
            m_old = m_sc[r]
            m_new = jnp.maximum(m_old, jnp.max(s, axis=-1, keepdims=True))
            alpha = jnp.exp(m_old - m_new)
            p = jnp.exp(s - m_new)
            l_sc[r] = alpha * l_sc[r] + jnp.sum(p, axis=-1, keepdims=True)
            acc_sc[r] = alpha * acc_sc[r] + _dot(p.astype(BF16), v)
            m_sc[r] = m_new
        return carry

    lax.fori_loop(0, (s0 + tq - 1) // tk + 1, body, 0)
    gl = gl_ref[...]
    for r in range(NSA_REP):
        gate = _sigmoid(gl[:, 3 * r + 1:3 * r + 2])
        o_ref[:, r * HEAD_DIM:(r + 1) * HEAD_DIM] = acc_sc[r] / l_sc[r] * gate


def _sel_attn(qr, ks, vs, proj, sel, B, S, tq, tk):
    T = B * S
    nqb = S // tq
    n_sel = S // SEL_BLOCK
    gw = NSA_REP * HEAD_DIM
    ebase = (np.arange(n_sel, dtype=np.int32)[:, None]
             - (np.arange(tk, dtype=np.int32)[None, :] // SEL_BLOCK))
    return pl.pallas_call(
        functools.partial(_sel_attn_kernel, tk=tk),
        grid=(B, NSA_KV_HEADS, nqb),
        in_specs=[
            pl.BlockSpec((tq, gw), lambda b, g, i: (b * nqb + i, g)),
            pl.BlockSpec((None, None, S, HEAD_DIM), lambda b, g, i: (b, g, 0, 0)),
            pl.BlockSpec((None, None, S, HEAD_DIM), lambda b, g, i: (b, g, 0, 0)),
            pl.BlockSpec((tq, 128), lambda b, g, i: (b * nqb + i, COL_GATE + g)),
            pl.BlockSpec((None, None, tq, n_sel), lambda b, g, i: (b, g, i, 0)),
            pl.BlockSpec((n_sel, tk), lambda b, g, i: (0, 0)),
        ],
        out_specs=pl.BlockSpec((tq, gw), lambda b, g, i: (b * nqb + i, g)),
        out_shape=jax.ShapeDtypeStruct((T, NSA_WIDTH), F32),
        scratch_shapes=[
            pltpu.VMEM((NSA_REP, tq, 1), F32),
            pltpu.VMEM((NSA_REP, tq, 1), F32),
            pltpu.VMEM((NSA_REP, tq, HEAD_DIM), F32),
        ],
        compiler_params=_cparams(3),
    )(qr, ks, vs, proj, sel, jnp.asarray(ebase))


def _win_attn_kernel(q_ref, k_ref, v_ref, gl_ref, o_ref, *, wk):
    i = pl.program_id(2)
    tq = q_ref.shape[0]
    s0 = i * tq
    t = s0 + lax.broadcasted_iota(jnp.int32, (tq, 1), 0)
    ws = pl.multiple_of(jnp.maximum(s0 - WINDOW, 0), tq)
    k = k_ref[pl.ds(ws, wk), :]
    v = v_ref[pl.ds(ws, wk), :]
    kp = ws + lax.broadcasted_iota(jnp.int32, (1, wk), 1)
    mask = (kp <= t) & (kp > t - WINDOW)
    gl = gl_ref[...]
    for r in range(NSA_REP):
        sl = slice(r * HEAD_DIM, (r + 1) * HEAD_DIM)
        s = jnp.where(mask, _dot_nt(q_ref[:, sl], k), NEG_BIG)
        m = jnp.max(s, axis=-1, keepdims=True)
        e = jnp.exp(s - m)
        p = e / jnp.sum(e, axis=-1, keepdims=True)
        gate = _sigmoid(gl[:, 3 * r + 2:3 * r + 3])
        o_ref[:, sl] = _dot(p.astype(BF16), v) * gate


def _win_attn(qr, kw, vw, proj, B, S, tq):
    T = B * S
    nqb = S // tq
    gw = NSA_REP * HEAD_DIM
    wk = min(WINDOW + tq, S)
    return pl.pallas_call(
        functools.partial(_win_attn_kernel, wk=wk),
        grid=(B, NSA_KV_HEADS, nqb),
        in_specs=[
            pl.BlockSpec((tq, gw), lambda b, g, i: (b * nqb + i, g)),
            pl.BlockSpec((None, None, S, HEAD_DIM), lambda b, g, i: (b, g, 0, 0)),
            pl.BlockSpec((None, None, S, HEAD_DIM), lambda b, g, i: (b, g, 0, 0)),
            pl.BlockSpec((tq, 128), lambda b, g, i: (b * nqb + i, COL_GATE + g)),
        ],
        out_specs=pl.BlockSpec((tq, gw), lambda b, g, i: (b * nqb + i, g)),
        out_shape=jax.ShapeDtypeStruct((T, NSA_WIDTH), F32),
        compiler_params=_cparams(3),
    )(qr, kw, vw, proj)


def _memkv_kernel(mem_ref, g_ref, w_ref, kg_ref, k_o, v_o):
    h = _rms(mem_ref[...], g_ref[...]).astype(BF16)
    kv = _dot(h, w_ref[...])
    kg = kg_ref[...]
    for hd in range(XATTN_HEADS):
        sl = slice(hd * HEAD_DIM, (hd + 1) * HEAD_DIM)
        sv = slice(XATTN_WIDTH + hd * HEAD_DIM, XATTN_WIDTH + (hd + 1) * HEAD_DIM)
        k_o[:, sl] = _rms(kv[:, sl], kg).astype(BF16)
        v_o[:, sl] = kv[:, sv].astype(BF16)


def _mem_kv(mem, g, wkv, k_g):
    B, M, D = mem.shape
    shp = jax.ShapeDtypeStruct((B, M, XATTN_WIDTH), BF16)
    spec = pl.BlockSpec((None, M, XATTN_WIDTH), lambda b: (b, 0, 0))
    return pl.pallas_call(
        _memkv_kernel,
        grid=(B,),
        in_specs=[
            pl.BlockSpec((None, M, D), lambda b: (b, 0, 0)),
            pl.BlockSpec((1, D), lambda b: (0, 0)),
            pl.BlockSpec((D, 2 * XATTN_WIDTH), lambda b: (0, 0)),
            pl.BlockSpec((1, HEAD_DIM), lambda b: (0, 0)),
        ],
        out_specs=[spec, spec],
        out_shape=[shp, shp],
        compiler_params=_cparams(1),
    )(mem, g, wkv, k_g)


def _xattn_kernel(q_ref, k_ref, v_ref, qg_ref, o_ref):
    scale = np.float32(HEAD_DIM ** -0.5)
    qg = qg_ref[...]
    for hd in range(XATTN_HEADS):
        sl = slice(hd * HEAD_DIM, (hd + 1) * HEAD_DIM)
        q = (_rms(q_ref[:, sl], qg) * scale).astype(BF16)
        s = _dot_nt(q, k_ref[:, sl])
        m = jnp.max(s, axis=-1, keepdims=True)
        e = jnp.exp(s - m)
        p = e / jnp.sum(e, axis=-1, keepdims=True)
        o_ref[:, sl] = _dot(p.astype(BF16), v_ref[:, sl]).astype(BF16)


def _xattn(proj, km, vm, q_g, B, S, tm):
    T = B * S
    nsb = S // tm
    M = km.shape[1]
    return pl.pallas_call(
        _xattn_kernel,
        grid=(T // tm,),
        in_specs=[
            pl.BlockSpec((tm, XATTN_WIDTH), lambda i: (i, COL_QX * 128 // XATTN_WIDTH)),
            pl.BlockSpec((None, M, XATTN_WIDTH), lambda i: (i // nsb, 0, 0)),
            pl.BlockSpec((None, M, XATTN_WIDTH), lambda i: (i // nsb, 0, 0)),
            pl.BlockSpec((1, HEAD_DIM), lambda i: (0, 0)),
        ],
        out_specs=pl.BlockSpec((tm, XATTN_WIDTH), lambda i: (i, 0)),
        out_shape=jax.ShapeDtypeStruct((T, XATTN_WIDTH), BF16),
        compiler_params=_cparams(1),
    )(proj, km, vm, q_g)


def _outproj_kernel(x_ref, ya_ref, oc_ref, os_ref, ow_ref, yc_ref, wo_ref, g_ref,
                    wrh_ref, wrl_ref, br_ref, x1_o, h2_o, ti_o, tg_o, *, n_experts):
    yb = (oc_ref[...] + os_ref[...] + ow_ref[...]).astype(BF16)
    x1 = x_ref[...]
    x1 += _dot(ya_ref[...], wo_ref[0:GMLP_WIDTH, :])
    x1 += _dot(yb, wo_ref[GMLP_WIDTH:GMLP_WIDTH + NSA_WIDTH, :])
    x1 += _dot(yc_ref[...], wo_ref[GMLP_WIDTH + NSA_WIDTH:, :])
    x1_o[...] = x1
    h2 = _rms(x1, g_ref[...])
    h2_o[...] = h2
    hi = h2.astype(BF16)
    lo = (h2 - hi.astype(F32)).astype(BF16)
    logits = (_dot(hi, wrh_ref[...]) + _dot(lo, wrh_ref[...]) + _dot(hi, wrl_ref[...])
              + br_ref[...])
    lane = lax.broadcasted_iota(jnp.int32, (1, 128), 1)
    lane_f = lane.astype(F32)
    lg = jnp.where(lane < n_experts, logits, -jnp.inf)
    ti = jnp.zeros(logits.shape, F32)
    tg = jnp.zeros(logits.shape, F32)
    denom = jnp.zeros((logits.shape[0], 1), F32)
    v0 = None
    for k in range(TOP_K):
        m = jnp.max(lg, axis=-1, keepdims=True)
        ix = jnp.min(jnp.where(lg == m, lane_f, 128.0), axis=-1, keepdims=True)
        if v0 is None:
            v0 = m
        e = jnp.exp(m - v0)
        denom += e
        ti = jnp.where(lane == k, ix, ti)
        tg = jnp.where(lane == k, e, tg)
        lg = jnp.where(lane_f == ix, -jnp.inf, lg)
    ti_o[...] = ti
    tg_o[...] = tg / denom


def _out_proj(x2, ya, oc, os_, ow, yc, wo, g, wr_hi, wr_lo, br, n_experts, tm):
    T, D = x2.shape
    row = lambda w: pl.BlockSpec((tm, w), lambda i: (i, 0))
    full = lambda a: pl.BlockSpec(a.shape, lambda i: (0,) * a.ndim)
    return pl.pallas_call(
        functools.partial(_outproj_kernel, n_experts=n_experts),
        grid=(T // tm,),
        in_specs=[row(D), row(GMLP_WIDTH), row(NSA_WIDTH), row(NSA_WIDTH), row(NSA_WIDTH),
                  row(XATTN_WIDTH), full(wo), full(g), full(wr_hi), full(wr_lo), full(br)],
        out_specs=[row(D), row(D), row(128), row(128)],
        out_shape=[
            jax.ShapeDtypeStruct((T, D), F32),
            jax.ShapeDtypeStruct((T, D), F32),
            jax.ShapeDtypeStruct((T, 128), F32),
            jax.ShapeDtypeStruct((T, 128), F32),
        ],
        compiler_params=_cparams(1),
    )(x2, ya, oc, os_, ow, yc, wo, g, wr_hi, wr_lo, br)


def _moe_kernel(be_ref, rt_hbm, h2_hbm, rg_ref, w1g_ref, w1l_ref, b1g_ref, b1l_ref,
                w2_ref, b2_ref, y_ref, idx_smem, xf_ref, xb_ref, sem_i, sem_g, *, nf):
    b = pl.program_id(0)
    f = pl.program_id(1)
    blk = xf_ref.shape[0]

    def row_copy(tok, r):
        return pltpu.make_async_copy(h2_hbm.at[pl.ds(tok, 1), :], xf_ref.at[pl.ds(r, 1), :], sem_g)

    @pl.when(f == 0)
    def _():
        cp = pltpu.make_async_copy(rt_hbm.at[b], idx_smem, sem_i)
        cp.start()
        cp.wait()

        def issue(r, c):
            row_copy(idx_smem[r], r).start()
            return c

        lax.fori_loop(0, blk, issue, 0)

        def drain(r, c):
            row_copy(0, r).wait()
            return c

        lax.fori_loop(0, blk, drain, 0)
        xb_ref[...] = xf_ref[...].astype(BF16)

    x = xb_ref[...]
    glu = jnp.minimum(_dot(x, w1g_ref[...]) + b1g_ref[...], SWIGLU_LIMIT)
    lin = jnp.clip(_dot(x, w1l_ref[...]) + b1l_ref[...], -SWIGLU_LIMIT, SWIGLU_LIMIT)
    act = glu * _sigmoid(SWIGLU_ALPHA * glu) * (lin + 1.0)
    contrib = _dot(act.astype(BF16), w2_ref[...])

    @pl.when(f == 0)
    def _():
        y_ref[...] = contrib

    @pl.when(f > 0)
    def _():
        y_ref[...] += contrib

    @pl.when(f == nf - 1)
    def _():
        y_ref[...] = (y_ref[...] + b2_ref[...]) * rg_ref[...]


def _moe_experts(block_e, row_tok2, h2, row_g, w1g, w1l, b1g, b1l, w2, b2, tf):
    n_blocks, blk = row_tok2.shape
    E, D, F = w1g.shape
    nf = F // tf
    grid_spec = pltpu.PrefetchScalarGridSpec(
        num_scalar_prefetch=1,
        grid=(n_blocks, nf),
        in_specs=[
            pl.BlockSpec(memory_space=pl.ANY),
            pl.BlockSpec(memory_space=pl.ANY),
            pl.BlockSpec((blk, 1), lambda b, f, be: (b, 0)),
            pl.BlockSpec((None, D, tf), lambda b, f, be: (be[b], 0, f)),
            pl.BlockSpec((None, D, tf), lambda b, f, be: (be[b], 0, f)),
            pl.BlockSpec((None, 1, tf), lambda b, f, be: (be[b], 0, f)),
            pl.BlockSpec((None, 1, tf), lambda b, f, be: (be[b], 0, f)),
            pl.BlockSpec((None, tf, D), lambda b, f, be: (be[b], f, 0)),
            pl.BlockSpec((None, 1, D), lambda b, f, be: (be[b], 0, 0)),
        ],
        out_specs=pl.BlockSpec((blk, D), lambda b, f, be: (b, 0)),
        scratch_shapes=[
            pltpu.SMEM((blk,), jnp.int32),
            pltpu.VMEM((blk, D), F32),
            pltpu.VMEM((blk, D), BF16),
            pltpu.SemaphoreType.DMA,
            pltpu.SemaphoreType.DMA,
        ],
    )
    return pl.pallas_call(
        functools.partial(_moe_kernel, nf=nf),
        grid_spec=grid_spec,
        out_shape=jax.ShapeDtypeStruct((n_blocks * blk, D), F32),
        compiler_params=_cparams(2),
    )(block_e, row_tok2, h2, row_g, w1g, w1l, b1g, b1l, w2, b2)


def _combine_kernel(pos_hbm, ys_hbm, x1_ref, o_ref, idx_smem, buf_ref, sem_i, sem_g):
    i = pl.program_id(0)
    tc = x1_ref.shape[0]
    n = idx_smem.shape[0]

    def row_copy(src, r):
        return pltpu.make_async_copy(ys_hbm.at[pl.ds(src, 1), :], buf_ref.at[pl.ds(r, 1), :], sem_g)

    cp = pltpu.make_async_copy(pos_hbm.at[i], idx_smem, sem_i)
    cp.start()
    cp.wait()

    def issue(r, c):
        row_copy(idx_smem[r], r).start()
        return c

    lax.fori_loop(0, n, issue, 0)

    def drain(r, c):
        row_copy(0, r).wait()
        return c

    lax.fori_loop(0, n, drain, 0)
    acc = x1_ref[...]
    for k in range(TOP_K):
        acc += buf_ref[k * tc:(k + 1) * tc, :]
    o_ref[...] = acc


def _combine(pos_tiles, ys, x1, tc):
    T, D = x1.shape
    return pl.pallas_call(
        _combine_kernel,
        grid=(T // tc,),
        in_specs=[
            pl.BlockSpec(memory_space=pl.ANY),
            pl.BlockSpec(memory_space=pl.ANY),
            pl.BlockSpec((tc, D), lambda i: (i, 0)),
        ],
        out_specs=pl.BlockSpec((tc, D), lambda i: (i, 0)),
        out_shape=jax.ShapeDtypeStruct((T, D), F32),
        scratch_shapes=[
            pltpu.SMEM((TOP_K * tc,), jnp.int32),
            pltpu.VMEM((TOP_K * tc, D), F32),
            pltpu.SemaphoreType.DMA,
            pltpu.SemaphoreType.DMA,
        ],
        compiler_params=_cparams(1),
    )(pos_tiles, ys, x1)


def _route(top_idx, top_gate, n_experts, blk):
    T = top_idx.shape[0]
    N = T * TOP_K
    e_flat = top_idx.reshape(N)
    g_flat = top_gate.reshape(N)
    tok_flat = jnp.repeat(jnp.arange(T, dtype=jnp.int32), TOP_K)
    onehot = (e_flat[:, None] == jnp.arange(n_experts, dtype=jnp.int32)[None, :]).astype(jnp.int32)
    csum = jnp.cumsum(onehot, axis=0)
    rank = jnp.sum(csum * onehot, axis=1) - 1
    counts = csum[-1]
    padded = ((counts + blk - 1) // blk) * blk
    pend = jnp.cumsum(padded)
    pstart = pend - padded
    dest = pstart[e_flat] + rank
    n_blocks = -(-(N + n_experts * (blk - 1)) // blk)
    rows = n_blocks * blk
    row_tok = jnp.zeros((rows,), jnp.int32).at[dest].set(tok_flat)
    row_g = jnp.zeros((rows,), F32).at[dest].set(g_flat)
    block_e = jnp.minimum(
        jnp.searchsorted(pend, jnp.arange(n_blocks, dtype=jnp.int32) * blk, side='right'),
        n_experts - 1).astype(jnp.int32)
    return dest, row_tok.reshape(n_blocks, blk), row_g.reshape(rows, 1), block_e


def _cmp_to_sel(nc, n_sel):
    ratio = SEL_BLOCK // CMP_STRIDE
    offs = (np.arange(ratio)[:, None] - np.arange(CMP_LEN // CMP_STRIDE)[None, :]).reshape(-1)
    diff = np.arange(nc)[:, None] - ratio * np.arange(n_sel)[None, :]
    return (diff[..., None] == offs).sum(-1).astype(np.float32)


def _pick(n, pref):
    for c in pref:
        if n % c == 0:
            return c
    return n


def kernel(x, mem, positions, mix_norm_g, mem_norm_g, w_in, w_out, gmlp_ws, gmlp_bs, gmlp_vg,
           nsa_cmp_pos, nsa_ck1, nsa_ck2, nsa_cv1, nsa_cv2, nsa_q_g, nsa_k_g,
           xattn_wkv, xattn_q_g, xattn_k_g, ffn_norm_g, w_router, b_router,
           w_e1, b_e1, w_e2, b_e2):
    B, S, D = x.shape
    T = B * S
    depth = w_in.shape[0]
    n_experts = w_router.shape[-1]
    NC = S // CMP_STRIDE
    n_sel = S // SEL_BLOCK
    tm = _pick(S, (512, 256, 128))
    tk = _pick(S, (512, 256, 128))

    half = HEAD_DIM // 2
    inv = ROPE_THETA ** (-jnp.arange(half, dtype=F32) * 2.0 / HEAD_DIM)
    ang = (positions.astype(F32).reshape(T, 1) * jnp.concatenate([inv, inv])[None, :])
    c2s = jnp.asarray(_cmp_to_sel(NC, n_sel), BF16)

    xs = x.reshape(T, D)
    for l in range(depth):
        wi = w_in[l]
        n_gate = 3 * NSA_REP
        zpad = jnp.zeros((D, 128 - n_gate), wi.dtype)
        w_r = jnp.concatenate(
            [wi[:, :3584], wi[:, 3608:4120],
             wi[:, 3584:3584 + n_gate], zpad, wi[:, 3584 + n_gate:3608], zpad], axis=1).astype(BF16)

        proj = _in_proj(xs, mix_norm_g[l][None, :], w_r, tm, PROJ_COLS // 2)

        bias_full = jnp.repeat(jnp.transpose(gmlp_bs[l]), HEAD_DIM, axis=1)
        y_a = _gmlp(proj, gmlp_ws[l], bias_full, gmlp_vg[l].reshape(1, GMLP_WIDTH), T, tm)

        qn, qr, ks, vs, kw, vw = _prep(proj, ang, nsa_q_g[l][None, :], nsa_k_g[l][None, :], B, S, tm)

        w1 = jnp.stack([nsa_ck1[l], nsa_cv1[l]]).astype(BF16)
        w2 = jnp.stack([nsa_ck2[l], nsa_cv2[l]]).astype(BF16)
        cmp_kv = _compress(proj.reshape(B, NC, CMP_STRIDE, PROJ_COLS), w1, w2,
                           nsa_cmp_pos[l], nsa_k_g[l][None, :], B, NC)

        o_c, sel = _cmp_attn(qn, cmp_kv, proj, c2s, B, S, Q_BLOCK)
        o_s = _sel_attn(qr, ks, vs, proj, sel, B, S, Q_BLOCK, tk)
        o_w = _win_attn(qr, kw, vw, proj, B, S, Q_BLOCK)

        km, vm = _mem_kv(mem, mem_norm_g[l][None, :], xattn_wkv[l].astype(BF16), xattn_k_g[l][None, :])
        y_c = _xattn(proj, km, vm, xattn_q_g[l][None, :], B, S, tm)

        wr = jnp.pad(w_router[l], ((0, 0), (0, 128 - n_experts)))
        wr_hi = wr.astype(BF16)
        wr_lo = (wr - wr_hi.astype(F32)).astype(BF16)
        br = jnp.pad(b_router[l], (0, 128 - n_experts))[None, :]
        x1, h2, ti, tg = _out_proj(xs, y_a, o_c, o_s, o_w, y_c, w_out[l].astype(BF16),
                                   ffn_norm_g[l][None, :], wr_hi, wr_lo, br, n_experts,
                                   _pick(S, (256, 128)))

        top_idx = ti[:, :TOP_K].astype(jnp.int32)
        dest, row_tok2, row_g, block_e = _route(top_idx, tg[:, :TOP_K], n_experts, MOE_BLOCK)

        w1g = w_e1[l][:, :, 0::2].astype(BF16)
        w1l = w_e1[l][:, :, 1::2].astype(BF16)
        b1g = b_e1[l][:, None, 0::2]
        b1l = b_e1[l][:, None, 1::2]
        ys = _moe_experts(block_e, row_tok2, h2, row_g, w1g, w1l, b1g, b1l,
                          w_e2[l].astype(BF16), b_e2[l][:, None, :], 512)

        tc = 128
        pos_tiles = dest.reshape(T // tc, tc, TOP_K).transpose(0, 2, 1).reshape(T // tc, TOP_K * tc)
        xs = _combine(pos_tiles, ys, x1, tc)
    return xs.reshape(B, S, D)
```

```python
import functools

import numpy as np
import jax
import jax.numpy as jnp
from jax import lax
from jax.experimental import pallas as pl
from jax.experimental.pallas import tpu as pltpu

F32 = jnp.float32
BF16 = jnp.bfloat16

HEAD_DIM = 128
GMLP_GROUPS = 4
NSA_HEADS = 8
NSA_KV_HEADS = 2
NSA_REP = NSA_HEADS // NSA_KV_HEADS
XATTN_HEADS = 4
GMLP_WIDTH = GMLP_GROUPS * HEAD_DIM
NSA_WIDTH = NSA_HEADS * HEAD_DIM
XATTN_WIDTH = XATTN_HEADS * HEAD_DIM
GMLP_CHUNK = 128
CMP_LEN = 32
CMP_STRIDE = 16
SEL_BLOCK = 64
SEL_TOPK = 16
WINDOW = 512
Q_BLOCK = 128
TOP_K = 4
SWIGLU_ALPHA = 1.702
SWIGLU_LIMIT = 7.0
MOE_BLOCK = 512
ROPE_THETA = 10000.0
EPS = 1e-6
NEG_BIG = -1e30

COL_UV = 0
COL_Q = 8
COL_KV = 16
COL_QX = 28
COL_GATE = 32
N_COL_BLOCKS = 34
PROJ_COLS = N_COL_BLOCKS * 128

VMEM_LIMIT = 56 * 1024 * 1024


def _cparams(n_axes):
    return pltpu.CompilerParams(
        dimension_semantics=("arbitrary",) * n_axes, vmem_limit_bytes=VMEM_LIMIT)


def _rms(x, g):
    ms = jnp.mean(x * x, axis=-1, keepdims=True)
    return x * lax.rsqrt(ms + EPS) * g


def _gelu_tanh(x):
    c = np.float32(np.sqrt(2.0 / np.pi))
    return 0.5 * x * (1.0 + jnp.tanh(c * (x + 0.044715 * (x * x * x))))


def _sigmoid(x):
    return 1.0 / (1.0 + jnp.exp(-x))


def _dot(a, b):
    return jnp.dot(a, b, preferred_element_type=F32)


def _dot_nt(a, b):
    return lax.dot_general(a, b, (((1,), (1,)), ((), ())), preferred_element_type=F32)


def _inproj_kernel(x_ref, g_ref, w_ref, o_ref, h_ref):
    @pl.when(pl.program_id(1) == 0)
    def _():
        h_ref[...] = _rms(x_ref[...], g_ref[...]).astype(BF16)

    o_ref[...] = _dot(h_ref[...], w_ref[...])


def _in_proj(x2, g, w_r, tm, tn):
    T, D = x2.shape
    NC = w_r.shape[1]
    return pl.pallas_call(
        _inproj_kernel,
        grid=(T // tm, NC // tn),
        in_specs=[
            pl.BlockSpec((tm, D), lambda i, j: (i, 0)),
            pl.BlockSpec((1, D), lambda i, j: (0, 0)),
            pl.BlockSpec((D, tn), lambda i, j: (0, j)),
        ],
        out_specs=pl.BlockSpec((tm, tn), lambda i, j: (i, j)),
        out_shape=jax.ShapeDtypeStruct((T, NC), F32),
        scratch_shapes=[pltpu.VMEM((tm, D), BF16)],
        compiler_params=_cparams(2),
    )(x2, g, w_r)


def _prep_kernel(q_ref, sel_ref, win_ref, ang_ref, qg_ref, kg_ref,
                 qn_o, qr_o, ks_o, vs_o, kw_o, vw_o):
    ang = ang_ref[...]
    lane = lax.broadcasted_iota(jnp.int32, (1, HEAD_DIM), 1)
    cosf = jnp.cos(ang)
    sinf = jnp.sin(ang) * jnp.where(lane < HEAD_DIM // 2, -1.0, 1.0)
    scale = np.float32(HEAD_DIM ** -0.5)

    def rope(x):
        return x * cosf + pltpu.roll(x, HEAD_DIM // 2, axis=1) * sinf

    qg = qg_ref[...]
    kg = kg_ref[...]
    for h in range(NSA_HEADS):
        sl = slice(h * HEAD_DIM, (h + 1) * HEAD_DIM)
        n = _rms(q_ref[:, sl], qg)
        qn_o[:, sl] = (n * scale).astype(BF16)
        qr_o[:, sl] = (rope(n) * scale).astype(BF16)
    for g in range(NSA_KV_HEADS):
        sl = slice(g * HEAD_DIM, (g + 1) * HEAD_DIM)
        sv = slice((NSA_KV_HEADS + g) * HEAD_DIM, (NSA_KV_HEADS + g + 1) * HEAD_DIM)
        ks_o[g] = rope(_rms(sel_ref[:, sl], kg)).astype(BF16)
        vs_o[g] = sel_ref[:, sv].astype(BF16)
        kw_o[g] = rope(_rms(win_ref[:, sl], kg)).astype(BF16)
        vw_o[g] = win_ref[:, sv].astype(BF16)


def _prep(proj, ang, q_g, k_g, B, S, tm):
    T = B * S
    nsb = S // tm
    kv_shape = jax.ShapeDtypeStruct((B, NSA_KV_HEADS, S, HEAD_DIM), BF16)
    kv_spec = pl.BlockSpec((None, NSA_KV_HEADS, tm, HEAD_DIM),
                           lambda i: (i // nsb, 0, i % nsb, 0))
    return pl.pallas_call(
        _prep_kernel,
        grid=(T // tm,),
        in_specs=[
            pl.BlockSpec((tm, NSA_WIDTH), lambda i: (i, COL_Q * 128 // NSA_WIDTH)),
            pl.BlockSpec((tm, 512), lambda i: (i, (COL_KV + 4) * 128 // 512)),
            pl.BlockSpec((tm, 512), lambda i: (i, (COL_KV + 8) * 128 // 512)),
            pl.BlockSpec((tm, HEAD_DIM), lambda i: (i, 0)),
            pl.BlockSpec((1, HEAD_DIM), lambda i: (0, 0)),
            pl.BlockSpec((1, HEAD_DIM), lambda i: (0, 0)),
        ],
        out_specs=[
            pl.BlockSpec((tm, NSA_WIDTH), lambda i: (i, 0)),
            pl.BlockSpec((tm, NSA_WIDTH), lambda i: (i, 0)),
            kv_spec, kv_spec, kv_spec, kv_spec,
        ],
        out_shape=[
            jax.ShapeDtypeStruct((T, NSA_WIDTH), BF16),
            jax.ShapeDtypeStruct((T, NSA_WIDTH), BF16),
            kv_shape, kv_shape, kv_shape, kv_shape,
        ],
        compiler_params=_cparams(1),
    )(proj, proj, proj, ang, q_g, k_g)


def _gmlp_kernel(uv_ref, ws_ref, bias_ref, gv_ref, o_ref, *, n_chunks):
    row = lax.broadcasted_iota(jnp.int32, (GMLP_CHUNK, GMLP_CHUNK), 0)
    col = lax.broadcasted_iota(jnp.int32, (GMLP_CHUNK, GMLP_CHUNK), 1)
    causal = col <= row
    for g in range(GMLP_GROUPS):
        sl = slice(g * HEAD_DIM, (g + 1) * HEAD_DIM)
        sv = slice(GMLP_WIDTH + g * HEAD_DIM, GMLP_WIDTH + (g + 1) * HEAD_DIM)
        w = jnp.where(causal, ws_ref[g], 0.0).astype(BF16)
        u = _gelu_tanh(uv_ref[:, sl])
        v = _rms(_gelu_tanh(uv_ref[:, sv]), gv_ref[:, sl]).astype(BF16)
        bias = bias_ref[:, sl]
        for c in range(n_chunks):
            rs = slice(c * GMLP_CHUNK, (c + 1) * GMLP_CHUNK)
            vs = _dot(w, v[rs]) + bias
            o_ref[rs, sl] = (u[rs] * vs).astype(BF16)


def _gmlp(proj, ws, bias_full, gv, T, tm):
    return pl.pallas_call(
        functools.partial(_gmlp_kernel, n_chunks=tm // GMLP_CHUNK),
        grid=(T // tm,),
        in_specs=[
            pl.BlockSpec((tm, 2 * GMLP_WIDTH), lambda i: (i, 0)),
            pl.BlockSpec((GMLP_GROUPS, GMLP_CHUNK, GMLP_CHUNK), lambda i: (0, 0, 0)),
            pl.BlockSpec((GMLP_CHUNK, GMLP_WIDTH), lambda i: (0, 0)),
            pl.BlockSpec((1, GMLP_WIDTH), lambda i: (0, 0)),
        ],
        out_specs=pl.BlockSpec((tm, GMLP_WIDTH), lambda i: (i, 0)),
        out_shape=jax.ShapeDtypeStruct((T, GMLP_WIDTH), BF16),
        compiler_params=_cparams(1),
    )(proj, ws, bias_full, gv)


def _compress_kernel(x_ref, w1_ref, w2_ref, pos_ref, kg_ref, o_ref):
    nc = x_ref.shape[0]
    kind = pl.program_id(1)
    acc_a = jnp.zeros((nc, HEAD_DIM), F32)
    acc_b = jnp.zeros((nc, HEAD_DIM), F32)
    for j in range(CMP_STRIDE):
        xj = x_ref[:, j, :]
        wa = w1_ref[j * HEAD_DIM:(j + 1) * HEAD_DIM, :]
        wb = w1_ref[(CMP_STRIDE + j) * HEAD_DIM:(CMP_STRIDE + j + 1) * HEAD_DIM, :]
        acc_a += _dot((xj + pos_ref[j:j + 1, :]).astype(BF16), wa)
        acc_b += _dot((xj + pos_ref[CMP_STRIDE + j:CMP_STRIDE + j + 1, :]).astype(BF16), wb)
    hidden = acc_a + pltpu.roll(acc_b, nc - 1, axis=0)
    out = _dot(_gelu_tanh(hidden).astype(BF16), w2_ref[...])
    o_ref[...] = jnp.where(kind == 0, _rms(out, kg_ref[...]), out).astype(BF16)


def _compress(proj4, w1, w2, cmp_pos, k_g, B, NC):
    return pl.pallas_call(
        _compress_kernel,
        grid=(B, 2, NSA_KV_HEADS),
        in_specs=[
            pl.BlockSpec((None, NC, CMP_STRIDE, HEAD_DIM),
                         lambda b, k, g: (b, 0, 0, COL_KV + 2 * k + g)),
            pl.BlockSpec((None, CMP_LEN * HEAD_DIM, HEAD_DIM), lambda b, k, g: (k, 0, 0)),
            pl.BlockSpec((None, HEAD_DIM, HEAD_DIM), lambda b, k, g: (k, 0, 0)),
            pl.BlockSpec((CMP_LEN, HEAD_DIM), lambda b, k, g: (0, 0)),
            pl.BlockSpec((1, HEAD_DIM), lambda b, k, g: (0, 0)),
        ],
        out_specs=pl.BlockSpec((None, None, None, NC, HEAD_DIM), lambda b, k, g: (b, k, g, 0, 0)),
        out_shape=jax.ShapeDtypeStruct((B, 2, NSA_KV_HEADS, NC, HEAD_DIM), BF16),
        compiler_params=_cparams(3),
    )(proj4, w1, w2, cmp_pos, k_g)


def _cmp_attn_kernel(q_ref, kc_ref, vc_ref, gl_ref, c2s_ref, oc_ref, sel_ref, q_sc, *, k_top):
    i = pl.program_id(2)
    tq = q_ref.shape[0]
    nc = kc_ref.shape[0]
    n_sel = sel_ref.shape[1]
    t = i * tq + lax.broadcasted_iota(jnp.int32, (tq, 1), 0)
    cmp_end = lax.broadcasted_iota(jnp.int32, (1, nc), 1) * CMP_STRIDE + (CMP_LEN - 1)
    bias = jnp.where(cmp_end <= t, 0.0, -jnp.inf)
    gl = gl_ref[...]
    for r in range(NSA_REP):
        q_sc[r * tq:(r + 1) * tq, :] = q_ref[:, r * HEAD_DIM:(r + 1) * HEAD_DIM]
    s = _dot_nt(q_sc[...], kc_ref[...]) + jnp.concatenate([bias] * NSA_REP, axis=0)
    m = jnp.max(s, axis=-1, keepdims=True)
    m = jnp.where(m > -jnp.inf, m, 0.0)
    e = jnp.exp(s - m)
    d = jnp.sum(e, axis=-1, keepdims=True)
    p = e * (1.0 / jnp.where(d > 0, d, 1.0))
    o = _dot(p.astype(BF16), vc_ref[...])
    pc_sum = jnp.zeros((tq, nc), F32)
    for r in range(NSA_REP):
        rs = slice(r * tq, (r + 1) * tq)
        gate = _sigmoid(gl[:, 3 * r:3 * r + 1])
        oc_ref[:, r * HEAD_DIM:(r + 1) * HEAD_DIM] = o[rs, :] * gate
        pc_sum += p[rs, :]
    hi = pc_sum.astype(BF16)
    lo = (pc_sum - hi.astype(F32)).astype(BF16)
    imp = _dot(hi, c2s_ref[...]) + _dot(lo, c2s_ref[...])

    sid = lax.broadcasted_iota(jnp.int32, (1, n_sel), 1)
    sid_f = sid.astype(F32)
    cur = t // SEL_BLOCK
    valid = sid * SEL_BLOCK <= t
    forced = (sid == 0) | (sid == cur) | (sid == cur - 1)
    score = jnp.where(valid, jnp.where(forced, jnp.inf, imp), -jnp.inf)
    sel = jnp.zeros((tq, n_sel), F32)
    for _ in range(k_top):
        m = jnp.max(score, axis=-1, keepdims=True)
        idx = jnp.min(jnp.where(score == m, sid_f, np.float32(n_sel)), axis=-1, keepdims=True)
        pick = sid_f == idx
        sel = jnp.where(pick, 1.0, sel)
        score = jnp.where(pick, -jnp.inf, score)
    sel_ref[...] = sel.astype(BF16)


def _cmp_attn(qn, cmp_kv, proj, c2s, B, S, tq):
    T = B * S
    nqb = S // tq
    NC = cmp_kv.shape[3]
    n_sel = S // SEL_BLOCK
    gw = NSA_REP * HEAD_DIM
    return pl.pallas_call(
        functools.partial(_cmp_attn_kernel, k_top=min(SEL_TOPK, n_sel)),
        grid=(B, NSA_KV_HEADS, nqb),
        in_specs=[
            pl.BlockSpec((tq, gw), lambda b, g, i: (b * nqb + i, g)),
            pl.BlockSpec((None, None, None, NC, HEAD_DIM), lambda b, g, i: (b, 0, g, 0, 0)),
            pl.BlockSpec((None, None, None, NC, HEAD_DIM), lambda b, g, i: (b, 1, g, 0, 0)),
            pl.BlockSpec((tq, 128), lambda b, g, i: (b * nqb + i, COL_GATE + g)),
            pl.BlockSpec((NC, n_sel), lambda b, g, i: (0, 0)),
        ],
        out_specs=[
            pl.BlockSpec((tq, gw), lambda b, g, i: (b * nqb + i, g)),
            pl.BlockSpec((None, None, tq, n_sel), lambda b, g, i: (b, g, i, 0)),
        ],
        out_shape=[
            jax.ShapeDtypeStruct((T, NSA_WIDTH), F32),
            jax.ShapeDtypeStruct((B, NSA_KV_HEADS, S, n_sel), BF16),
        ],
        scratch_shapes=[pltpu.VMEM((NSA_REP * tq, HEAD_DIM), BF16)],
        compiler_params=_cparams(3),
    )(qn, cmp_kv, cmp_kv, proj, c2s)


def _sel_attn_kernel(q_ref, k_ref, v_ref, gl_ref, sel_ref, eb_ref, o_ref,
                     q_sc, m_sc, l_sc, acc_sc, *, tk):
    i = pl.program_id(2)
    tq = q_ref.shape[0]
    s0 = i * tq
    t = s0 + lax.broadcasted_iota(jnp.int32, (tq, 1), 0)
    for r in range(NSA_REP):
        q_sc[r * tq:(r + 1) * tq, :] = q_ref[:, r * HEAD_DIM:(r + 1) * HEAD_DIM]
    m_sc[...] = jnp.full(m_sc.shape, NEG_BIG, F32)
    l_sc[...] = jnp.zeros(l_sc.shape, F32)
    acc_sc[...] = jnp.zeros(acc_sc.shape, F32)
    key0 = lax.broadcasted_iota(jnp.int32, (1, tk), 1)
    blocks_per_tile = tk // SEL_BLOCK

    def body(j, carry):
        koff = pl.multiple_of(j * tk, tk)
        k = k_ref[pl.ds(koff, tk), :]
        v = v_ref[pl.ds(koff, tk), :]
        expand = jnp.where(eb_ref[...] == j * blocks_per_tile, 1.0, 0.0).astype(BF16)
        sel_tok = _dot(sel_ref[...], expand)
        bias = jnp.where((sel_tok > 0.5) & (key0 + koff <= t), 0.0, NEG_BIG)
        s = _dot_nt(q_sc[...], k) + jnp.concatenate([bias] * NSA_REP, axis=0)
        m_old = m_sc[...]
        m_new = jnp.maximum(m_old, jnp.max(s, axis=-1, keepdims=True))
        alpha = jnp.exp(m_old - m_new)
        p = jnp.exp(s - jnp.concatenate([m_new] * (tk // 128), axis=1))
        l_sc[...] = alpha * l_sc[...] + jnp.sum(p, axis=-1, keepdims=True)
        acc_sc[...] = alpha * acc_sc[...] + _dot(p.astype(BF16), v)
        m_sc[...] = m_new
        return carry

    lax.fori_loop(0, (s0 + tq - 1) // tk + 1, body, 0)
    gl = gl_ref[...]
    for r in range(NSA_REP):
        rs = slice(r * tq, (r + 1) * tq)
        gate = _sigmoid(gl[:, 3 * r + 1:3 * r + 2])
        o_ref[:, r * HEAD_DIM:(r + 1) * HEAD_DIM] = acc_sc[rs, :] / l_sc[rs, :] * gate


def _sel_attn(qr, ks, vs, proj, sel, B, S, tq, tk):
    T = B * S
    nqb = S // tq
    n_sel = S // SEL_BLOCK
    gw = NSA_REP * HEAD_DIM
    ebase = (np.arange(n_sel, dtype=np.int32)[:, None]
             - (np.arange(tk, dtype=np.int32)[None, :] // SEL_BLOCK))
    return pl.pallas_call(
        functools.partial(_sel_attn_kernel, tk=tk),
        grid=(B, NSA_KV_HEADS, nqb),
        in_specs=[
            pl.BlockSpec((tq, gw), lambda b, g, i: (b * nqb + i, g)),
            pl.BlockSpec((None, None, S, HEAD_DIM), lambda b, g, i: (b, g, 0, 0)),
            pl.BlockSpec((None, None, S, HEAD_DIM), lambda b, g, i: (b, g, 0, 0)),
            pl.BlockSpec((tq, 128), lambda b, g, i: (b * nqb + i, COL_GATE + g)),
            pl.BlockSpec((None, None, tq, n_sel), lambda b, g, i: (b, g, i, 0)),
            pl.BlockSpec((n_sel, tk), lambda b, g, i: (0, 0)),
        ],
        out_specs=pl.BlockSpec((tq, gw), lambda b, g, i: (b * nqb + i, g)),
        out_shape=jax.ShapeDtypeStruct((T, NSA_WIDTH), F32),
        scratch_shapes=[
            pltpu.VMEM((NSA_REP * tq, HEAD_DIM), BF16),
            pltpu.VMEM((NSA_REP * tq, 128), F32),
            pltpu.VMEM((NSA_REP * tq, 128), F32),
            pltpu.VMEM((NSA_REP * tq, HEAD_DIM), F32),
        ],
        compiler_params=_cparams(3),
    )(qr, ks, vs, proj, sel, jnp.asarray(ebase))


def _win_attn_kernel(q_ref, k_ref, v_ref, gl_ref, o_ref, *, wk):
    i = pl.program_id(2)
    tq = q_ref.shape[0]
    s0 = i * tq
    t = s0 + lax.broadcasted_iota(jnp.int32, (tq, 1), 0)
    ws = pl.multiple_of(jnp.maximum(s0 - WINDOW, 0), tq)
    k = k_ref[pl.ds(ws, wk), :]
    v = v_ref[pl.ds(ws, wk), :]
    kp = ws + lax.broadcasted_iota(jnp.int32, (1, wk), 1)
    mask = (kp <= t) & (kp > t - WINDOW)
    gl = gl_ref[...]
    for r in range(NSA_REP):
        sl = slice(r * HEAD_DIM, (r + 1) * HEAD_DIM)
        s = jnp.where(mask, _dot_nt(q_ref[:, sl], k), NEG_BIG)
        m = jnp.max(s, axis=-1, keepdims=True)
        e = jnp.exp(s - m)
        p = e * (1.0 / jnp.sum(e, axis=-1, keepdims=True))
        gate = _sigmoid(gl[:, 3 * r + 2:3 * r + 3])
        o_ref[:, sl] = _dot(p.astype(BF16), v) * gate


def _win_attn(qr, kw, vw, proj, B, S, tq):
    T = B * S
    nqb = S // tq
    gw = NSA_REP * HEAD_DIM
    wk = min(WINDOW + tq, S)
    return pl.pallas_call(
        functools.partial(_win_attn_kernel, wk=wk),
        grid=(B, NSA_KV_HEADS, nqb),
        in_specs=[
            pl.BlockSpec((tq, gw), lambda b, g, i: (b * nqb + i, g)),
            pl.BlockSpec((None, None, S, HEAD_DIM), lambda b, g, i: (b, g, 0, 0)),
            pl.BlockSpec((None, None, S, HEAD_DIM), lambda b, g, i: (b, g, 0, 0)),
            pl.BlockSpec((tq, 128), lambda b, g, i: (b * nqb + i, COL_GATE + g)),
        ],
        out_specs=pl.BlockSpec((tq, gw), lambda b, g, i: (b * nqb + i, g)),
        out_shape=jax.ShapeDtypeStruct((T, NSA_WIDTH), F32),
        compiler_params=_cparams(3),
    )(qr, kw, vw, proj)


def _memkv_kernel(mem_ref, g_ref, w_ref, kg_ref, k_o, v_o):
    h = _rms(mem_ref[...], g_ref[...]).astype(BF16)
    kv = _dot(h, w_ref[...])
    kg = kg_ref[...]
    for hd in range(XATTN_HEADS):
        sl = slice(hd * HEAD_DIM, (hd + 1) * HEAD_DIM)
        sv = slice(XATTN_WIDTH + hd * HEAD_DIM, XATTN_WIDTH + (hd + 1) * HEAD_DIM)
        k_o[:, sl] = _rms(kv[:, sl], kg).astype(BF16)
        v_o[:, sl] = kv[:, sv].astype(BF16)


def _mem_kv(mem, g, wkv, k_g):
    B, M, D = mem.shape
    shp = jax.ShapeDtypeStruct((B, M, XATTN_WIDTH), BF16)
    spec = pl.BlockSpec((None, M, XATTN_WIDTH), lambda b: (b, 0, 0))
    return pl.pallas_call(
        _memkv_kernel,
        grid=(B,),
        in_specs=[
            pl.BlockSpec((None, M, D), lambda b: (b, 0, 0)),
            pl.BlockSpec((1, D), lambda b: (0, 0)),
            pl.BlockSpec((D, 2 * XATTN_WIDTH), lambda b: (0, 0)),
            pl.BlockSpec((1, HEAD_DIM), lambda b: (0, 0)),
        ],
        out_specs=[spec, spec],
        out_shape=[shp, shp],
        compiler_params=_cparams(1),
    )(mem, g, wkv, k_g)


def _xattn_kernel(q_ref, k_ref, v_ref, qg_ref, o_ref):
    scale = np.float32(HEAD_DIM ** -0.5)
    qg = qg_ref[...]
    for hd in range(XATTN_HEADS):
        sl = slice(hd * HEAD_DIM, (hd + 1) * HEAD_DIM)
        q = (_rms(q_ref[:, sl], qg) * scale).astype(BF16)
        s = _dot_nt(q, k_ref[:, sl])
        m = jnp.max(s, axis=-1, keepdims=True)
        e = jnp.exp(s - m)
        p = e * (1.0 / jnp.sum(e, axis=-1, keepdims=True))
        o_ref[:, sl] = _dot(p.astype(BF16), v_ref[:, sl]).astype(BF16)


def _xattn(proj, km, vm, q_g, B, S, tm):
    T = B * S
    nsb = S // tm
    M = km.shape[1]
    return pl.pallas_call(
        _xattn_kernel,
        grid=(T // tm,),
        in_specs=[
            pl.BlockSpec((tm, XATTN_WIDTH), lambda i: (i, COL_QX * 128 // XATTN_WIDTH)),
            pl.BlockSpec((None, M, XATTN_WIDTH), lambda i: (i // nsb, 0, 0)),
            pl.BlockSpec((None, M, XATTN_WIDTH), lambda i: (i // nsb, 0, 0)),
            pl.BlockSpec((1, HEAD_DIM), lambda i: (0, 0)),
        ],
        out_specs=pl.BlockSpec((tm, XATTN_WIDTH), lambda i: (i, 0)),
        out_shape=jax.ShapeDtypeStruct((T, XATTN_WIDTH), BF16),
        compiler_params=_cparams(1),
    )(proj, km, vm, q_g)


def _outproj_kernel(x_ref, ya_ref, oc_ref, os_ref, ow_ref, yc_ref, wo_ref, g_ref,
                    wrh_ref, wrl_ref, br_ref, x1_o, h2_o, ti_o, tg_o, *, n_experts):
    yb = (oc_ref[...] + os_ref[...] + ow_ref[...]).astype(BF16)
    x1 = x_ref[...]
    x1 += _dot(ya_ref[...], wo_ref[0:GMLP_WIDTH, :])
    x1 += _dot(yb, wo_ref[GMLP_WIDTH:GMLP_WIDTH + NSA_WIDTH, :])
    x1 += _dot(yc_ref[...], wo_ref[GMLP_WIDTH + NSA_WIDTH:, :])
    x1_o[...] = x1
    h2 = _rms(x1, g_ref[...])
    h2_o[...] = h2
    hi = h2.astype(BF16)
    lo = (h2 - hi.astype(F32)).astype(BF16)
    logits = (_dot(hi, wrh_ref[...]) + _dot(lo, wrh_ref[...]) + _dot(hi, wrl_ref[...])
              + br_ref[...])
    lane = lax.broadcasted_iota(jnp.int32, (1, 128), 1)
    lane_f = lane.astype(F32)
    lg = jnp.where(lane < n_experts, logits, -jnp.inf)
    ti = jnp.zeros(logits.shape, F32)
    tg = jnp.zeros(logits.shape, F32)
    denom = jnp.zeros((logits.shape[0], 1), F32)
    v0 = None
    for k in range(TOP_K):
        m = jnp.max(lg, axis=-1, keepdims=True)
        ix = jnp.min(jnp.where(lg == m, lane_f, 128.0), axis=-1, keepdims=True)
        if v0 is None:
            v0 = m
        e = jnp.exp(m - v0)
        denom += e
        ti = jnp.where(lane == k, ix, ti)
        tg = jnp.where(lane == k, e, tg)
        lg = jnp.where(lane_f == ix, -jnp.inf, lg)
    ti_o[...] = ti
    tg_o[...] = tg / denom


def _out_proj(x2, ya, oc, os_, ow, yc, wo, g, wr_hi, wr_lo, br, n_experts, tm):
    T, D = x2.shape
    row = lambda w: pl.BlockSpec((tm, w), lambda i: (i, 0))
    full = lambda a: pl.BlockSpec(a.shape, lambda i: (0,) * a.ndim)
    return pl.pallas_call(
        functools.partial(_outproj_kernel, n_experts=n_experts),
        grid=(T // tm,),
        in_specs=[row(D), row(GMLP_WIDTH), row(NSA_WIDTH), row(NSA_WIDTH), row(NSA_WIDTH),
                  row(XATTN_WIDTH), full(wo), full(g), full(wr_hi), full(wr_lo), full(br)],
        out_specs=[row(D), row(D), row(128), row(128)],
        out_shape=[
            jax.ShapeDtypeStruct((T, D), F32),
            jax.ShapeDtypeStruct((T, D), F32),
            jax.ShapeDtypeStruct((T, 128), F32),
            jax.ShapeDtypeStruct((T, 128), F32),
        ],
        compiler_params=_cparams(1),
    )(x2, ya, oc, os_, ow, yc, wo, g, wr_hi, wr_lo, br)


def _deinterleave_kernel(w_ref, p_ref, g_o, l_o):
    perm = p_ref[...]
    for c in range(w_ref.shape[1] // 256):
        t = _dot(w_ref[:, c * 256:(c + 1) * 256].astype(BF16), perm)
        g_o[:, c * 128:(c + 1) * 128] = t[:, :128].astype(BF16)
        l_o[:, c * 128:(c + 1) * 128] = t[:, 128:].astype(BF16)


def _deinterleave(w, tr, tcw):
    R, C = w.shape
    src = np.arange(256)
    dst = np.where(src % 2 == 0, src // 2, 128 + src // 2)
    perm = np.zeros((256, 256), np.float32)
    perm[src, dst] = 1.0
    out = jax.ShapeDtypeStruct((R, C // 2), BF16)
    return pl.pallas_call(
        _deinterleave_kernel,
        grid=(R // tr, C // tcw),
        in_specs=[
            pl.BlockSpec((tr, tcw), lambda i, j: (i, j)),
            pl.BlockSpec((256, 256), lambda i, j: (0, 0)),
        ],
        out_specs=[pl.BlockSpec((tr, tcw // 2), lambda i, j: (i, j))] * 2,
        out_shape=[out, out],
        compiler_params=_cparams(2),
    )(w, jnp.asarray(perm, BF16))


def _moe_kernel(be_ref, nu_ref, rt_hbm, h2_hbm, rg_ref, w1g_ref, w1l_ref, b1g_ref, b1l_ref,
                w2_ref, b2_ref, y_ref, idx_smem, xf_ref, xb_ref, sem_i, sem_g, *, nf):
    b = pl.program_id(0)
    f = pl.program_id(1)
    blk = xb_ref.shape[0]
    n_used = nu_ref[0]
    slot = b % 2

    def row_copy(tok, r, sl):
        return pltpu.make_async_copy(h2_hbm.at[pl.ds(tok, 1), :], xf_ref.at[sl, pl.ds(r, 1), :],
                                     sem_g.at[sl])

    def start_gather(blk_id, sl):
        cp = pltpu.make_async_copy(rt_hbm.at[blk_id], idx_smem.at[sl], sem_i)
        cp.start()
        cp.wait()

        def issue(r, c):
            row_copy(idx_smem[sl, r], r, sl).start()
            return c

        lax.fori_loop(0, blk, issue, 0, unroll=8)

    @pl.when(b < n_used)
    def _():
        @pl.when((b == 0) & (f == 0))
        def _():
            start_gather(0, 0)

        @pl.when(f == 0)
        def _():
            def drain(r, c):
                row_copy(0, r, slot).wait()
                return c

            lax.fori_loop(0, blk, drain, 0, unroll=8)
            xb_ref[...] = xf_ref[slot].astype(BF16)

        @pl.when((f == min(1, nf - 1)) & (b + 1 < n_used))
        def _():
            start_gather(b + 1, 1 - slot)

        x = xb_ref[...]
        glu = jnp.minimum(_dot(x, w1g_ref[...]) + b1g_ref[...], SWIGLU_LIMIT)
        lin = jnp.clip(_dot(x, w1l_ref[...]) + b1l_ref[...], -SWIGLU_LIMIT, SWIGLU_LIMIT)
        act = glu * _sigmoid(SWIGLU_ALPHA * glu) * (lin + 1.0)
        contrib = _dot(act.astype(BF16), w2_ref[...])

        @pl.when(f == 0)
        def _():
            y_ref[...] = contrib

        @pl.when(f > 0)
        def _():
            y_ref[...] += contrib

        @pl.when(f == nf - 1)
        def _():
            y_ref[...] = (y_ref[...] + b2_ref[...]) * rg_ref[...]

    @pl.when((b >= n_used) & (f == 0))
    def _():
        y_ref[...] = jnp.zeros(y_ref.shape, F32)


def _moe_experts(block_e, n_used, row_tok2, h2, row_g, w1g, w1l, b1g, b1l, w2, b2, tf):
    n_blocks, blk = row_tok2.shape
    E, D, F = w1g.shape
    nf = F // tf

    def bb(b, nu):
        return jnp.minimum(b, nu[0] - 1)

    def ff(b, f, nu):
        return jnp.where(b < nu[0], f, nf - 1)

    grid_spec = pltpu.PrefetchScalarGridSpec(
        num_scalar_prefetch=2,
        grid=(n_blocks, nf),
        in_specs=[
            pl.BlockSpec(memory_space=pl.ANY),
            pl.BlockSpec(memory_space=pl.ANY),
            pl.BlockSpec((blk, 1), lambda b, f, be, nu: (bb(b, nu), 0)),
            pl.BlockSpec((None, D, tf), lambda b, f, be, nu: (be[bb(b, nu)], 0, ff(b, f, nu))),
            pl.BlockSpec((None, D, tf), lambda b, f, be, nu: (be[bb(b, nu)], 0, ff(b, f, nu))),
            pl.BlockSpec((None, 1, tf), lambda b, f, be, nu: (be[bb(b, nu)], 0, ff(b, f, nu))),
            pl.BlockSpec((None, 1, tf), lambda b, f, be, nu: (be[bb(b, nu)], 0, ff(b, f, nu))),
            pl.BlockSpec((None, tf, D), lambda b, f, be, nu: (be[bb(b, nu)], ff(b, f, nu), 0)),
            pl.BlockSpec((None, 1, D), lambda b, f, be, nu: (be[bb(b, nu)], 0, 0)),
        ],
        out_specs=pl.BlockSpec((blk, D), lambda b, f, be, nu: (b, 0)),
        scratch_shapes=[
            pltpu.SMEM((2, blk), jnp.int32),
            pltpu.VMEM((2, blk, D), F32),
            pltpu.VMEM((blk, D), BF16),
            pltpu.SemaphoreType.DMA,
            pltpu.SemaphoreType.DMA((2,)),
        ],
    )
    return pl.pallas_call(
        functools.partial(_moe_kernel, nf=nf),
        grid_spec=grid_spec,
        out_shape=jax.ShapeDtypeStruct((n_blocks * blk, D), F32),
        compiler_params=_cparams(2),
    )(block_e, n_used, row_tok2, h2, row_g, w1g, w1l, b1g, b1l, w2, b2)


def _combine_kernel(pos_hbm, ys_hbm, x1_ref, o_ref, idx_smem, buf_ref, sem_i, sem_g):
    i = pl.program_id(0)
    tc = x1_ref.shape[0]
    n = idx_smem.shape[1]
    slot = i % 2

    def row_copy(src, r, sl):
        return pltpu.make_async_copy(ys_hbm.at[pl.ds(src, 1), :], buf_ref.at[sl, pl.ds(r, 1), :],
                                     sem_g.at[sl])

    def start_gather(step, sl):
        cp = pltpu.make_async_copy(pos_hbm.at[step], idx_smem.at[sl], sem_i)
        cp.start()
        cp.wait()

        def issue(r, c):
            row_copy(idx_smem[sl, r], r, sl).start()
            return c

        lax.fori_loop(0, n, issue, 0, unroll=8)

    @pl.when(i == 0)
    def _():
        start_gather(0, 0)

    @pl.when(i + 1 < pl.num_programs(0))
    def _():
        start_gather(i + 1, 1 - slot)

    def drain(r, c):
        row_copy(0, r, slot).wait()
        return c

    lax.fori_loop(0, n, drain, 0, unroll=8)
    acc = x1_ref[...]
    for k in range(TOP_K):
        acc += buf_ref[slot, k * tc:(k + 1) * tc, :]
    o_ref[...] = acc


def _combine(pos_tiles, ys, x1, tc):
    T, D = x1.shape
    return pl.pallas_call(
        _combine_kernel,
        grid=(T // tc,),
        in_specs=[
            pl.BlockSpec(memory_space=pl.ANY),
            pl.BlockSpec(memory_space=pl.ANY),
            pl.BlockSpec((tc, D), lambda i: (i, 0)),
        ],
        out_specs=pl.BlockSpec((tc, D), lambda i: (i, 0)),
        out_shape=jax.ShapeDtypeStruct((T, D), F32),
        scratch_shapes=[
            pltpu.SMEM((2, TOP_K * tc), jnp.int32),
            pltpu.VMEM((2, TOP_K * tc, D), F32),
            pltpu.SemaphoreType.DMA,
            pltpu.SemaphoreType.DMA((2,)),
        ],
        compiler_params=_cparams(1),
    )(pos_tiles, ys, x1)


def _route(top_idx, top_gate, n_experts, blk):
    T = top_idx.shape[0]
    N = T * TOP_K
    e_flat = top_idx.reshape(N)
    g_flat = top_gate.reshape(N)
    tok_flat = jnp.repeat(jnp.arange(T, dtype=jnp.int32), TOP_K)
    onehot = (e_flat[:, None] == jnp.arange(n_experts, dtype=jnp.int32)[None, :]).astype(jnp.int32)
    csum = jnp.cumsum(onehot, axis=0)
    rank = jnp.sum(csum * onehot, axis=1) - 1
    counts = csum[-1]
    padded = ((counts + blk - 1) // blk) * blk
    pend = jnp.cumsum(padded)
    pstart = pend - padded
    dest = pstart[e_flat] + rank
    n_blocks = -(-(N + n_experts * (blk - 1)) // blk)
    rows = n_blocks * blk
    packed = jnp.zeros((rows, 2), F32).at[dest].set(
        jnp.stack([tok_flat.astype(F32), g_flat], axis=1))
    row_tok = packed[:, 0].astype(jnp.int32)
    row_g = packed[:, 1]
    starts = jnp.arange(n_blocks, dtype=jnp.int32) * blk
    block_e = jnp.minimum(jnp.sum((pend[None, :] <= starts[:, None]).astype(jnp.int32), axis=1),
                          n_experts - 1).astype(jnp.int32)
    n_used = (pend[-1:] // blk).astype(jnp.int32)
    return dest, row_tok.reshape(n_blocks, blk), row_g.reshape(rows, 1), block_e, n_used


def _cmp_to_sel(nc, n_sel):
    ratio = SEL_BLOCK // CMP_STRIDE
    offs = (np.arange(ratio)[:, None] - np.arange(CMP_LEN // CMP_STRIDE)[None, :]).reshape(-1)
    diff = np.arange(nc)[:, None] - ratio * np.arange(n_sel)[None, :]
    return (diff[..., None] == offs).sum(-1).astype(np.float32)


def _pick(n, pref):
    for c in pref:
        if n % c == 0:
            return c
    return n


def kernel(x, mem, positions, mix_norm_g, mem_norm_g, w_in, w_out, gmlp_ws, gmlp_bs, gmlp_vg,
           nsa_cmp_pos, nsa_ck1, nsa_ck2, nsa_cv1, nsa_cv2, nsa_q_g, nsa_k_g,
           xattn_wkv, xattn_q_g, xattn_k_g, ffn_norm_g, w_router, b_router,
           w_e1, b_e1, w_e2, b_e2):
    B, S, D = x.shape
    T = B * S
    depth = w_in.shape[0]
    n_experts = w_router.shape[-1]
    NC = S // CMP_STRIDE
    n_sel = S // SEL_BLOCK
    tm = _pick(S, (512, 256, 128))
    tk = _pick(S, (512, 256, 128))

    half = HEAD_DIM // 2
    inv = ROPE_THETA ** (-jnp.arange(half, dtype=F32) * 2.0 / HEAD_DIM)
    ang = (positions.astype(F32).reshape(T, 1) * jnp.concatenate([inv, inv])[None, :])
    c2s = jnp.asarray(_cmp_to_sel(NC, n_sel), BF16)

    xs = x.reshape(T, D)
    for l in range(depth):
        wi = w_in[l]
        n_gate = 3 * NSA_REP
        zpad = jnp.zeros((D, 128 - n_gate), wi.dtype)
        w_r = jnp.concatenate(
            [wi[:, :3584], wi[:, 3608:4120],
             wi[:, 3584:3584 + n_gate], zpad, wi[:, 3584 + n_gate:3608], zpad], axis=1).astype(BF16)

        proj = _in_proj(xs, mix_norm_g[l][None, :], w_r, tm, PROJ_COLS // 2)

        bias_full = jnp.repeat(jnp.transpose(gmlp_bs[l]), HEAD_DIM, axis=1)
        y_a = _gmlp(proj, gmlp_ws[l], bias_full, gmlp_vg[l].reshape(1, GMLP_WIDTH), T, tm)

        qn, qr, ks, vs, kw, vw = _prep(proj, ang, nsa_q_g[l][None, :], nsa_k_g[l][None, :], B, S, tm)

        w1 = jnp.stack([nsa_ck1[l], nsa_cv1[l]]).astype(BF16)
        w2 = jnp.stack([nsa_ck2[l], nsa_cv2[l]]).astype(BF16)
        cmp_kv = _compress(proj.reshape(B, NC, CMP_STRIDE, PROJ_COLS), w1, w2,
                           nsa_cmp_pos[l], nsa_k_g[l][None, :], B, NC)

        o_c, sel = _cmp_attn(qn, cmp_kv, proj, c2s, B, S, Q_BLOCK)
        o_s = _sel_attn(qr, ks, vs, proj, sel, B, S, Q_BLOCK, tk)
        o_w = _win_attn(qr, kw, vw, proj, B, S, Q_BLOCK)

        km, vm = _mem_kv(mem, mem_norm_g[l][None, :], xattn_wkv[l].astype(BF16), xattn_k_g[l][None, :])
        y_c = _xattn(proj, km, vm, xattn_q_g[l][None, :], B, S, tm)

        wr = jnp.pad(w_router[l], ((0, 0), (0, 128 - n_experts)))
        wr_hi = wr.astype(BF16)
        wr_lo = (wr - wr_hi.astype(F32)).astype(BF16)
        br = jnp.pad(b_router[l], (0, 128 - n_experts))[None, :]
        x1, h2, ti, tg = _out_proj(xs, y_a, o_c, o_s, o_w, y_c, w_out[l].astype(BF16),
                                   ffn_norm_g[l][None, :], wr_hi, wr_lo, br, n_experts,
                                   _pick(S, (256, 128)))

        top_idx = ti[:, :TOP_K].astype(jnp.int32)
        dest, row_tok2, row_g, block_e, n_used = _route(top_idx, tg[:, :TOP_K], n_experts, MOE_BLOCK)

        d_ff = w_e2.shape[2]
        w1g, w1l = _deinterleave(w_e1[l].reshape(n_experts * D, 2 * d_ff), 512, _pick(2 * d_ff, (2048, 1024, 512, 256)))
        w1g = w1g.reshape(n_experts, D, d_ff)
        w1l = w1l.reshape(n_experts, D, d_ff)
        b1g = b_e1[l][:, None, 0::2]
        b1l = b_e1[l][:, None, 1::2]
        ys = _moe_experts(block_e, n_used, row_tok2, h2, row_g, w1g, w1l, b1g, b1l,
                          w_e2[l].astype(BF16), b_e2[l][:, None, :], 512)

        tc = 128
        pos_tiles = dest.reshape(T // tc, tc, TOP_K).transpose(0, 2, 1).reshape(T // tc, TOP_K * tc)
        xs = _combine(pos_tiles, ys, x1, tc)
    return xs.reshape(B, S, D)
```

```python
import functools

import numpy as np
import jax
import jax.numpy as jnp
from jax import lax
from jax.experimental import pallas as pl
from jax.experimental.pallas import tpu as pltpu

F32 = jnp.float32
BF16 = jnp.bfloat16

HEAD_DIM = 128
GMLP_GROUPS = 4
NSA_HEADS = 8
NSA_KV_HEADS = 2
NSA_REP = NSA_HEADS // NSA_KV_HEADS
XATTN_HEADS = 4
GMLP_WIDTH = GMLP_GROUPS * HEAD_DIM
NSA_WIDTH = NSA_HEADS * HEAD_DIM
XATTN_WIDTH = XATTN_HEADS * HEAD_DIM
GMLP_CHUNK = 128
CMP_LEN = 32
CMP_STRIDE = 16
SEL_BLOCK = 64
SEL_TOPK = 16
WINDOW = 512
Q_BLOCK = 128
TOP_K = 4
SWIGLU_ALPHA = 1.702
SWIGLU_LIMIT = 7.0
MOE_BLOCK = 512
ROPE_THETA = 10000.0
EPS = 1e-6
NEG_BIG = -1e30
MASK_BIG = 2.0 ** 100

COL_UV = 0
COL_Q = 8
COL_KV = 16
COL_QX = 28
COL_GATE = 32
N_COL_BLOCKS = 34
PROJ_COLS = N_COL_BLOCKS * 128

VMEM_LIMIT = 56 * 1024 * 1024


def _cparams(n_axes):
    return pltpu.CompilerParams(
        dimension_semantics=("arbitrary",) * n_axes, vmem_limit_bytes=VMEM_LIMIT)


def _rms(x, g):
    ms = jnp.mean(x * x, axis=-1, keepdims=True)
    return x * lax.rsqrt(ms + EPS) * g


def _gelu_tanh(x):
    c = np.float32(np.sqrt(2.0 / np.pi))
    return 0.5 * x * (1.0 + jnp.tanh(c * (x + 0.044715 * (x * x * x))))


def _sigmoid(x):
    return 1.0 / (1.0 + jnp.exp(-x))


def _dot(a, b):
    return jnp.dot(a, b, preferred_element_type=F32)


def _dot_nt(a, b):
    return lax.dot_general(a, b, (((1,), (1,)), ((), ())), preferred_element_type=F32)


def _inproj_kernel(x_ref, g_ref, w_ref, o_ref, h_ref):
    @pl.when(pl.program_id(1) == 0)
    def _():
        h_ref[...] = _rms(x_ref[...], g_ref[...]).astype(BF16)

    o_ref[...] = _dot(h_ref[...], w_ref[...])


def _in_proj(x2, g, w_r, tm, tn):
    T, D = x2.shape
    NC = w_r.shape[1]
    return pl.pallas_call(
        _inproj_kernel,
        grid=(T // tm, NC // tn),
        in_specs=[
            pl.BlockSpec((tm, D), lambda i, j: (i, 0)),
            pl.BlockSpec((1, D), lambda i, j: (0, 0)),
            pl.BlockSpec((D, tn), lambda i, j: (0, j)),
        ],
        out_specs=pl.BlockSpec((tm, tn), lambda i, j: (i, j)),
        out_shape=jax.ShapeDtypeStruct((T, NC), F32),
        scratch_shapes=[pltpu.VMEM((tm, D), BF16)],
        compiler_params=_cparams(2),
    )(x2, g, w_r)


def _prep_kernel(q_ref, sel_ref, win_ref, ang_ref, qg_ref, kg_ref,
                 qn_o, qr_o, ks_o, vs_o, kw_o, vw_o):
    ang = ang_ref[...]
    lane = lax.broadcasted_iota(jnp.int32, (1, HEAD_DIM), 1)
    cosf = jnp.cos(ang)
    sinf = jnp.sin(ang) * jnp.where(lane < HEAD_DIM // 2, -1.0, 1.0)
    scale = np.float32(HEAD_DIM ** -0.5 * np.log2(np.e))

    def rope(x):
        return x * cosf + pltpu.roll(x, HEAD_DIM // 2, axis=1) * sinf

    qg = qg_ref[...]
    kg = kg_ref[...]
    for h in range(NSA_HEADS):
        sl = slice(h * HEAD_DIM, (h + 1) * HEAD_DIM)
        n = _rms(q_ref[:, sl], qg)
        qn_o[:, sl] = (n * scale).astype(BF16)
        qr_o[:, sl] = (rope(n) * scale).astype(BF16)
    for g in range(NSA_KV_HEADS):
        sl = slice(g * HEAD_DIM, (g + 1) * HEAD_DIM)
        sv = slice((NSA_KV_HEADS + g) * HEAD_DIM, (NSA_KV_HEADS + g + 1) * HEAD_DIM)
        ks_o[g] = rope(_rms(sel_ref[:, sl], kg)).astype(BF16)
        vs_o[g] = sel_ref[:, sv].astype(BF16)
        kw_o[g] = rope(_rms(win_ref[:, sl], kg)).astype(BF16)
        vw_o[g] = win_ref[:, sv].astype(BF16)


def _prep(proj, ang, q_g, k_g, B, S, tm):
    T = B * S
    nsb = S // tm
    kv_shape = jax.ShapeDtypeStruct((B, NSA_KV_HEADS, S, HEAD_DIM), BF16)
    kv_spec = pl.BlockSpec((None, NSA_KV_HEADS, tm, HEAD_DIM),
                           lambda i: (i // nsb, 0, i % nsb, 0))
    return pl.pallas_call(
        _prep_kernel,
        grid=(T // tm,),
        in_specs=[
            pl.BlockSpec((tm, NSA_WIDTH), lambda i: (i, COL_Q * 128 // NSA_WIDTH)),
            pl.BlockSpec((tm, 512), lambda i: (i, (COL_KV + 4) * 128 // 512)),
            pl.BlockSpec((tm, 512), lambda i: (i, (COL_KV + 8) * 128 // 512)),
            pl.BlockSpec((tm, HEAD_DIM), lambda i: (i, 0)),
            pl.BlockSpec((1, HEAD_DIM), lambda i: (0, 0)),
            pl.BlockSpec((1, HEAD_DIM), lambda i: (0, 0)),
        ],
        out_specs=[
            pl.BlockSpec((tm, NSA_WIDTH), lambda i: (i, 0)),
            pl.BlockSpec((tm, NSA_WIDTH), lambda i: (i, 0)),
            kv_spec, kv_spec, kv_spec, kv_spec,
        ],
        out_shape=[
            jax.ShapeDtypeStruct((T, NSA_WIDTH), BF16),
            jax.ShapeDtypeStruct((T, NSA_WIDTH), BF16),
            kv_shape, kv_shape, kv_shape, kv_shape,
        ],
        compiler_params=_cparams(1),
    )(proj, proj, proj, ang, q_g, k_g)


def _gmlp_kernel(uv_ref, ws_ref, bias_ref, gv_ref, o_ref, *, n_chunks):
    row = lax.broadcasted_iota(jnp.int32, (GMLP_CHUNK, GMLP_CHUNK), 0)
    col = lax.broadcasted_iota(jnp.int32, (GMLP_CHUNK, GMLP_CHUNK), 1)
    causal = col <= row
    for g in range(GMLP_GROUPS):
        sl = slice(g * HEAD_DIM, (g + 1) * HEAD_DIM)
        sv = slice(GMLP_WIDTH + g * HEAD_DIM, GMLP_WIDTH + (g + 1) * HEAD_DIM)
        w = jnp.where(causal, ws_ref[g], 0.0).astype(BF16)
        u = _gelu_tanh(uv_ref[:, sl])
        v = _rms(_gelu_tanh(uv_ref[:, sv]), gv_ref[:, sl]).astype(BF16)
        bias = bias_ref[:, sl]
        for c in range(n_chunks):
            rs = slice(c * GMLP_CHUNK, (c + 1) * GMLP_CHUNK)
            vs = _dot(w, v[rs]) + bias
            o_ref[rs, sl] = (u[rs] * vs).astype(BF16)


def _gmlp(proj, ws, bias_full, gv, T, tm):
    return pl.pallas_call(
        functools.partial(_gmlp_kernel, n_chunks=tm // GMLP_CHUNK),
        grid=(T // tm,),
        in_specs=[
            pl.BlockSpec((tm, 2 * GMLP_WIDTH), lambda i: (i, 0)),
            pl.BlockSpec((GMLP_GROUPS, GMLP_CHUNK, GMLP_CHUNK), lambda i: (0, 0, 0)),
            pl.BlockSpec((GMLP_CHUNK, GMLP_WIDTH), lambda i: (0, 0)),
            pl.BlockSpec((1, GMLP_WIDTH), lambda i: (0, 0)),
        ],
        out_specs=pl.BlockSpec((tm, GMLP_WIDTH), lambda i: (i, 0)),
        out_shape=jax.ShapeDtypeStruct((T, GMLP_WIDTH), BF16),
        compiler_params=_cparams(1),
    )(proj, ws, bias_full, gv)


def _compress_kernel(x_ref, w1_ref, w2_ref, pos_ref, kg_ref, o_ref):
    nc = x_ref.shape[0]
    kind = pl.program_id(1)
    acc_a = jnp.zeros((nc, HEAD_DIM), F32)
    acc_b = jnp.zeros((nc, HEAD_DIM), F32)
    for j in range(CMP_STRIDE):
        xj = x_ref[:, j, :]
        wa = w1_ref[j * HEAD_DIM:(j + 1) * HEAD_DIM, :]
        wb = w1_ref[(CMP_STRIDE + j) * HEAD_DIM:(CMP_STRIDE + j + 1) * HEAD_DIM, :]
        acc_a += _dot((xj + pos_ref[j:j + 1, :]).astype(BF16), wa)
        acc_b += _dot((xj + pos_ref[CMP_STRIDE + j:CMP_STRIDE + j + 1, :]).astype(BF16), wb)
    hidden = acc_a + pltpu.roll(acc_b, nc - 1, axis=0)
    out = _dot(_gelu_tanh(hidden).astype(BF16), w2_ref[...])
    o_ref[...] = jnp.where(kind == 0, _rms(out, kg_ref[...]), out).astype(BF16)


def _compress(proj4, w1, w2, cmp_pos, k_g, B, NC):
    return pl.pallas_call(
        _compress_kernel,
        grid=(B, 2, NSA_KV_HEADS),
        in_specs=[
            pl.BlockSpec((None, NC, CMP_STRIDE, HEAD_DIM),
                         lambda b, k, g: (b, 0, 0, COL_KV + 2 * k + g)),
            pl.BlockSpec((None, CMP_LEN * HEAD_DIM, HEAD_DIM), lambda b, k, g: (k, 0, 0)),
            pl.BlockSpec((None, HEAD_DIM, HEAD_DIM), lambda b, k, g: (k, 0, 0)),
            pl.BlockSpec((CMP_LEN, HEAD_DIM), lambda b, k, g: (0, 0)),
            pl.BlockSpec((1, HEAD_DIM), lambda b, k, g: (0, 0)),
        ],
        out_specs=pl.BlockSpec((None, None, None, NC, HEAD_DIM), lambda b, k, g: (b, k, g, 0, 0)),
        out_shape=jax.ShapeDtypeStruct((B, 2, NSA_KV_HEADS, NC, HEAD_DIM), BF16),
        compiler_params=_cparams(3),
    )(proj4, w1, w2, cmp_pos, k_g)


def _cmp_attn_kernel(q_ref, kc_ref, vc_ref, gl_ref, c2st_ref, oc_ref, sel_ref, q_sc, *, k_top):
    i = pl.program_id(2)
    tq = q_ref.shape[0]
    nc = kc_ref.shape[0]
    n_sel = c2st_ref.shape[0]
    t = i * tq + lax.broadcasted_iota(jnp.int32, (tq, 1), 0)
    cmp_end = lax.broadcasted_iota(jnp.int32, (1, nc), 1) * CMP_STRIDE + (CMP_LEN - 1)
    bias = jnp.where(cmp_end <= t, 0.0, -jnp.inf)
    gl = gl_ref[...]
    for r in range(NSA_REP):
        q_sc[r * tq:(r + 1) * tq, :] = q_ref[:, r * HEAD_DIM:(r + 1) * HEAD_DIM]
    s = _dot_nt(q_sc[...], kc_ref[...]) + jnp.concatenate([bias] * NSA_REP, axis=0)
    m = jnp.max(s, axis=-1, keepdims=True)
    m = jnp.where(m > -jnp.inf, m, 0.0)
    e = jnp.exp2(s - m)
    d = jnp.sum(e, axis=-1, keepdims=True)
    p = e * (1.0 / jnp.where(d > 0, d, 1.0))
    o = _dot(p.astype(BF16), vc_ref[...])
    pc_sum = jnp.zeros((tq, nc), F32)
    for r in range(NSA_REP):
        rs = slice(r * tq, (r + 1) * tq)
        gate = _sigmoid(gl[:, 3 * r:3 * r + 1])
        oc_ref[:, r * HEAD_DIM:(r + 1) * HEAD_DIM] = o[rs, :] * gate
        pc_sum += p[rs, :]
    hi = pc_sum.astype(BF16)
    lo = (pc_sum - hi.astype(F32)).astype(BF16)
    imp = _dot_nt(c2st_ref[...], hi) + _dot_nt(c2st_ref[...], lo)

    sid = lax.broadcasted_iota(jnp.int32, (n_sel, 1), 0)
    sid_f = sid.astype(F32)
    tt = i * tq + lax.broadcasted_iota(jnp.int32, (1, tq), 1)
    cur = tt // SEL_BLOCK
    valid = sid * SEL_BLOCK <= tt
    forced = (sid == 0) | (sid == cur) | (sid == cur - 1)
    score = jnp.where(valid, jnp.where(forced, jnp.inf, imp), -jnp.inf)
    sel = jnp.zeros((n_sel, tq), F32)
    for _ in range(k_top):
        m = jnp.max(score, axis=0, keepdims=True)
        idx = jnp.min(jnp.where(score == m, sid_f, np.float32(n_sel)), axis=0, keepdims=True)
        pick = sid_f == idx
        sel = jnp.where(pick, 1.0, sel)
        score = jnp.where(pick, -jnp.inf, score)
    selm = jnp.transpose(sel - 1.0)
    pad = sel_ref.shape[1] - n_sel
    if pad:
        selm = jnp.concatenate([selm, jnp.full((tq, pad), -1.0, F32)], axis=1)
    sel_ref[...] = selm


def _cmp_attn(qn, cmp_kv, proj, c2st, B, S, tq):
    T = B * S
    nqb = S // tq
    NC = cmp_kv.shape[3]
    n_sel = S // SEL_BLOCK
    n_sel_pad = -(-n_sel // 128) * 128
    gw = NSA_REP * HEAD_DIM
    return pl.pallas_call(
        functools.partial(_cmp_attn_kernel, k_top=min(SEL_TOPK, n_sel)),
        grid=(B, NSA_KV_HEADS, nqb),
        in_specs=[
            pl.BlockSpec((tq, gw), lambda b, g, i: (b * nqb + i, g)),
            pl.BlockSpec((None, None, None, NC, HEAD_DIM), lambda b, g, i: (b, 0, g, 0, 0)),
            pl.BlockSpec((None, None, None, NC, HEAD_DIM), lambda b, g, i: (b, 1, g, 0, 0)),
            pl.BlockSpec((tq, 128), lambda b, g, i: (b * nqb + i, COL_GATE + g)),
            pl.BlockSpec((n_sel, NC), lambda b, g, i: (0, 0)),
        ],
        out_specs=[
            pl.BlockSpec((tq, gw), lambda b, g, i: (b * nqb + i, g)),
            pl.BlockSpec((None, None, tq, n_sel_pad), lambda b, g, i: (b, g, i, 0)),
        ],
        out_shape=[
            jax.ShapeDtypeStruct((T, NSA_WIDTH), F32),
            jax.ShapeDtypeStruct((B, NSA_KV_HEADS, S, n_sel_pad), F32),
        ],
        scratch_shapes=[pltpu.VMEM((NSA_REP * tq, HEAD_DIM), BF16)],
        compiler_params=_cparams(3),
    )(qn, cmp_kv, cmp_kv, proj, c2st)


def _sel_attn_kernel(q_ref, k_ref, v_ref, gl_ref, sel_ref, e0_ref, o_ref,
                     q_sc, m_sc, l_sc, acc_sc, *, tk):
    i = pl.program_id(2)
    tq = q_ref.shape[0]
    s0 = i * tq
    t = s0 + lax.broadcasted_iota(jnp.int32, (tq, 1), 0)
    for r in range(NSA_REP):
        q_sc[r * tq:(r + 1) * tq, :] = q_ref[:, r * HEAD_DIM:(r + 1) * HEAD_DIM]
    m_sc[...] = jnp.full(m_sc.shape, NEG_BIG, F32)
    l_sc[...] = jnp.zeros(l_sc.shape, F32)
    acc_sc[...] = jnp.zeros(acc_sc.shape, F32)
    key0 = lax.broadcasted_iota(jnp.int32, (1, tk), 1)
    blocks_per_tile = tk // SEL_BLOCK

    def step(j, diagonal):
        koff = pl.multiple_of(j * tk, tk)
        k = k_ref[pl.ds(koff, tk), :]
        v = v_ref[pl.ds(koff, tk), :]
        c0 = j * blocks_per_tile
        slab = sel_ref[:, pl.ds(pl.multiple_of((c0 // 128) * 128, 128), 128)]
        rolled = pltpu.roll(slab, (128 - c0 % 128) % 128, axis=1)
        bias = _dot(rolled.astype(BF16), e0_ref[...])
        if diagonal:
            bias = jnp.where(key0 + koff <= t, bias, -MASK_BIG)
        s = _dot_nt(q_sc[...], k) + jnp.concatenate([bias] * NSA_REP, axis=0)
        m_old = m_sc[...]
        m_new = jnp.maximum(m_old, jnp.max(s, axis=-1, keepdims=True))
        alpha = jnp.exp2(m_old - m_new)
        p = jnp.exp2(s - jnp.concatenate([m_new] * (tk // 128), axis=1))
        l_sc[...] = alpha * l_sc[...] + jnp.sum(p, axis=-1, keepdims=True)
        acc_sc[...] = alpha * acc_sc[...] + _dot(p.astype(BF16), v)
        m_sc[...] = m_new

    def body(j, carry):
        step(j, False)
        return carry

    n_full = s0 // tk
    lax.fori_loop(0, n_full, body, 0)
    step(n_full, True)
    gl = gl_ref[...]
    for r in range(NSA_REP):
        rs = slice(r * tq, (r + 1) * tq)
        gate = _sigmoid(gl[:, 3 * r + 1:3 * r + 2])
        o_ref[:, r * HEAD_DIM:(r + 1) * HEAD_DIM] = acc_sc[rs, :] / l_sc[rs, :] * gate


def _sel_attn(qr, ks, vs, proj, sel, B, S, tq, tk):
    T = B * S
    nqb = S // tq
    n_sel = S // SEL_BLOCK
    gw = NSA_REP * HEAD_DIM
    n_sel_pad = sel.shape[-1]
    e0 = np.where(np.arange(128)[:, None] == np.arange(tk)[None, :] // SEL_BLOCK, MASK_BIG, 0.0)
    return pl.pallas_call(
        functools.partial(_sel_attn_kernel, tk=tk),
        grid=(B, NSA_KV_HEADS, nqb),
        in_specs=[
            pl.BlockSpec((tq, gw), lambda b, g, i: (b * nqb + i, g)),
            pl.BlockSpec((None, None, S, HEAD_DIM), lambda b, g, i: (b, g, 0, 0)),
            pl.BlockSpec((None, None, S, HEAD_DIM), lambda b, g, i: (b, g, 0, 0)),
            pl.BlockSpec((tq, 128), lambda b, g, i: (b * nqb + i, COL_GATE + g)),
            pl.BlockSpec((None, None, tq, n_sel_pad), lambda b, g, i: (b, g, i, 0)),
            pl.BlockSpec((128, tk), lambda b, g, i: (0, 0)),
        ],
        out_specs=pl.BlockSpec((tq, gw), lambda b, g, i: (b * nqb + i, g)),
        out_shape=jax.ShapeDtypeStruct((T, NSA_WIDTH), F32),
        scratch_shapes=[
            pltpu.VMEM((NSA_REP * tq, HEAD_DIM), BF16),
            pltpu.VMEM((NSA_REP * tq, 128), F32),
            pltpu.VMEM((NSA_REP * tq, 128), F32),
            pltpu.VMEM((NSA_REP * tq, HEAD_DIM), F32),
        ],
        compiler_params=_cparams(3),
    )(qr, ks, vs, proj, sel, jnp.asarray(e0, BF16))


def _win_attn_kernel(q_ref, k_ref, v_ref, gl_ref, o_ref, q_sc, *, wk):
    i = pl.program_id(2)
    tq = q_ref.shape[0]
    s0 = i * tq
    t = s0 + lax.broadcasted_iota(jnp.int32, (tq, 1), 0)
    for r in range(NSA_REP):
        q_sc[r * tq:(r + 1) * tq, :] = q_ref[:, r * HEAD_DIM:(r + 1) * HEAD_DIM]
    ws = pl.multiple_of(jnp.maximum(s0 - WINDOW, 0), tq)
    k = k_ref[pl.ds(ws, wk), :]
    v = v_ref[pl.ds(ws, wk), :]
    kp = ws + lax.broadcasted_iota(jnp.int32, (1, wk), 1)
    bias = jnp.where(kp <= t, jnp.where(kp > t - WINDOW, 0.0, NEG_BIG), NEG_BIG)
    s = _dot_nt(q_sc[...], k) + jnp.concatenate([bias] * NSA_REP, axis=0)
    m = jnp.max(s, axis=-1, keepdims=True)
    e = jnp.exp2(s - m)
    p = e * (1.0 / jnp.sum(e, axis=-1, keepdims=True))
    o = _dot(p.astype(BF16), v)
    gl = gl_ref[...]
    for r in range(NSA_REP):
        gate = _sigmoid(gl[:, 3 * r + 2:3 * r + 3])
        o_ref[:, r * HEAD_DIM:(r + 1) * HEAD_DIM] = o[r * tq:(r + 1) * tq, :] * gate


def _win_attn(qr, kw, vw, proj, B, S, tq):
    T = B * S
    nqb = S // tq
    gw = NSA_REP * HEAD_DIM
    wk = min(WINDOW + tq, S)
    return pl.pallas_call(
        functools.partial(_win_attn_kernel, wk=wk),
        grid=(B, NSA_KV_HEADS, nqb),
        in_specs=[
            pl.BlockSpec((tq, gw), lambda b, g, i: (b * nqb + i, g)),
            pl.BlockSpec((None, None, S, HEAD_DIM), lambda b, g, i: (b, g, 0, 0)),
            pl.BlockSpec((None, None, S, HEAD_DIM), lambda b, g, i: (b, g, 0, 0)),
            pl.BlockSpec((tq, 128), lambda b, g, i: (b * nqb + i, COL_GATE + g)),
        ],
        out_specs=pl.BlockSpec((tq, gw), lambda b, g, i: (b * nqb + i, g)),
        out_shape=jax.ShapeDtypeStruct((T, NSA_WIDTH), F32),
        scratch_shapes=[pltpu.VMEM((NSA_REP * tq, HEAD_DIM), BF16)],
        compiler_params=_cparams(3),
    )(qr, kw, vw, proj)


def _memkv_kernel(mem_ref, g_ref, w_ref, kg_ref, k_o, v_o):
    h = _rms(mem_ref[...], g_ref[...]).astype(BF16)
    kv = _dot(h, w_ref[...])
    kg = kg_ref[...]
    for hd in range(XATTN_HEADS):
        sl = slice(hd * HEAD_DIM, (hd + 1) * HEAD_DIM)
        sv = slice(XATTN_WIDTH + hd * HEAD_DIM, XATTN_WIDTH + (hd + 1) * HEAD_DIM)
        k_o[:, sl] = _rms(kv[:, sl], kg).astype(BF16)
        v_o[:, sl] = kv[:, sv].astype(BF16)


def _mem_kv(mem, g, wkv, k_g):
    B, M, D = mem.shape
    shp = jax.ShapeDtypeStruct((B, M, XATTN_WIDTH), BF16)
    spec = pl.BlockSpec((None, M, XATTN_WIDTH), lambda b: (b, 0, 0))
    return pl.pallas_call(
        _memkv_kernel,
        grid=(B,),
        in_specs=[
            pl.BlockSpec((None, M, D), lambda b: (b, 0, 0)),
            pl.BlockSpec((1, D), lambda b: (0, 0)),
            pl.BlockSpec((D, 2 * XATTN_WIDTH), lambda b: (0, 0)),
            pl.BlockSpec((1, HEAD_DIM), lambda b: (0, 0)),
        ],
        out_specs=[spec, spec],
        out_shape=[shp, shp],
        compiler_params=_cparams(1),
    )(mem, g, wkv, k_g)


def _xattn_kernel(q_ref, k_ref, v_ref, qg_ref, o_ref):
    scale = np.float32(HEAD_DIM ** -0.5)
    qg = qg_ref[...]
    for hd in range(XATTN_HEADS):
        sl = slice(hd * HEAD_DIM, (hd + 1) * HEAD_DIM)
        q = (_rms(q_ref[:, sl], qg) * scale).astype(BF16)
        s = _dot_nt(q, k_ref[:, sl])
        m = jnp.max(s, axis=-1, keepdims=True)
        e = jnp.exp(s - m)
        p = e * (1.0 / jnp.sum(e, axis=-1, keepdims=True))
        o_ref[:, sl] = _dot(p.astype(BF16), v_ref[:, sl]).astype(BF16)


def _xattn(proj, km, vm, q_g, B, S, tm):
    T = B * S
    nsb = S // tm
    M = km.shape[1]
    return pl.pallas_call(
        _xattn_kernel,
        grid=(T // tm,),
        in_specs=[
            pl.BlockSpec((tm, XATTN_WIDTH), lambda i: (i, COL_QX * 128 // XATTN_WIDTH)),
            pl.BlockSpec((None, M, XATTN_WIDTH), lambda i: (i // nsb, 0, 0)),
            pl.BlockSpec((None, M, XATTN_WIDTH), lambda i: (i // nsb, 0, 0)),
            pl.BlockSpec((1, HEAD_DIM), lambda i: (0, 0)),
        ],
        out_specs=pl.BlockSpec((tm, XATTN_WIDTH), lambda i: (i, 0)),
        out_shape=jax.ShapeDtypeStruct((T, XATTN_WIDTH), BF16),
        compiler_params=_cparams(1),
    )(proj, km, vm, q_g)


def _outproj_kernel(x_ref, ya_ref, oc_ref, os_ref, ow_ref, yc_ref, wo_ref, g_ref,
                    wrh_ref, wrl_ref, br_ref, x1_o, h2_o, ti_o, tg_o, *, n_experts):
    yb = (oc_ref[...] + os_ref[...] + ow_ref[...]).astype(BF16)
    x1 = x_ref[...]
    x1 += _dot(ya_ref[...], wo_ref[0:GMLP_WIDTH, :])
    x1 += _dot(yb, wo_ref[GMLP_WIDTH:GMLP_WIDTH + NSA_WIDTH, :])
    x1 += _dot(yc_ref[...], wo_ref[GMLP_WIDTH + NSA_WIDTH:, :])
    x1_o[...] = x1
    h2 = _rms(x1, g_ref[...])
    h2_o[...] = h2
    hi = h2.astype(BF16)
    lo = (h2 - hi.astype(F32)).astype(BF16)
    logits = (_dot(hi, wrh_ref[...]) + _dot(lo, wrh_ref[...]) + _dot(hi, wrl_ref[...])
              + br_ref[...])
    lane = lax.broadcasted_iota(jnp.int32, (1, 128), 1)
    lane_f = lane.astype(F32)
    lg = jnp.where(lane < n_experts, logits, -jnp.inf)
    ti = jnp.zeros(logits.shape, F32)
    tg = jnp.zeros(logits.shape, F32)
    denom = jnp.zeros((logits.shape[0], 1), F32)
    v0 = None
    for k in range(TOP_K):
        m = jnp.max(lg, axis=-1, keepdims=True)
        ix = jnp.min(jnp.where(lg == m, lane_f, 128.0), axis=-1, keepdims=True)
        if v0 is None:
            v0 = m
        e = jnp.exp(m - v0)
        denom += e
        ti = jnp.where(lane == k, ix, ti)
        tg = jnp.where(lane == k, e, tg)
        lg = jnp.where(lane_f == ix, -jnp.inf, lg)
    ti_o[...] = ti
    tg_o[...] = tg / denom


def _out_proj(x2, ya, oc, os_, ow, yc, wo, g, wr_hi, wr_lo, br, n_experts, tm):
    T, D = x2.shape
    row = lambda w: pl.BlockSpec((tm, w), lambda i: (i, 0))
    full = lambda a: pl.BlockSpec(a.shape, lambda i: (0,) * a.ndim)
    return pl.pallas_call(
        functools.partial(_outproj_kernel, n_experts=n_experts),
        grid=(T // tm,),
        in_specs=[row(D), row(GMLP_WIDTH), row(NSA_WIDTH), row(NSA_WIDTH), row(NSA_WIDTH),
                  row(XATTN_WIDTH), full(wo), full(g), full(wr_hi), full(wr_lo), full(br)],
        out_specs=[row(D), row(D), row(128), row(128)],
        out_shape=[
            jax.ShapeDtypeStruct((T, D), F32),
            jax.ShapeDtypeStruct((T, D), F32),
            jax.ShapeDtypeStruct((T, 128), F32),
            jax.ShapeDtypeStruct((T, 128), F32),
        ],
        compiler_params=_cparams(1),
    )(x2, ya, oc, os_, ow, yc, wo, g, wr_hi, wr_lo, br)


def _deinterleave_kernel(w_ref, p_ref, g_o, l_o):
    perm = p_ref[...]
    for c in range(w_ref.shape[1] // 256):
        t = _dot(w_ref[:, c * 256:(c + 1) * 256].astype(BF16), perm)
        g_o[:, c * 128:(c + 1) * 128] = t[:, :128].astype(BF16)
        l_o[:, c * 128:(c + 1) * 128] = t[:, 128:].astype(BF16)


def _deinterleave(w, tr, tcw):
    R, C = w.shape
    src = np.arange(256)
    dst = np.where(src % 2 == 0, src // 2, 128 + src // 2)
    perm = np.zeros((256, 256), np.float32)
    perm[src, dst] = 1.0
    out = jax.ShapeDtypeStruct((R, C // 2), BF16)
    return pl.pallas_call(
        _deinterleave_kernel,
        grid=(R // tr, C // tcw),
        in_specs=[
            pl.BlockSpec((tr, tcw), lambda i, j: (i, j)),
            pl.BlockSpec((256, 256), lambda i, j: (0, 0)),
        ],
        out_specs=[pl.BlockSpec((tr, tcw // 2), lambda i, j: (i, j))] * 2,
        out_shape=[out, out],
        compiler_params=_cparams(2),
    )(w, jnp.asarray(perm, BF16))


def _moe_kernel(be_ref, nu_ref, rt_hbm, h2_hbm, rg_ref, w1g_ref, w1l_ref, b1g_ref, b1l_ref,
                w2_ref, b2_ref, y_ref, idx_smem, xf_ref, xb_ref, sem_i, sem_g, *, nf):
    b = pl.program_id(0)
    f = pl.program_id(1)
    blk = xb_ref.shape[0]
    rows_per_step = blk // nf
    n_used = nu_ref[0]
    slot = b % 2
    last = b == n_used - 1

    def row_copy(tok, r, sl):
        return pltpu.make_async_copy(h2_hbm.at[pl.ds(tok, 1), :], xf_ref.at[sl, pl.ds(r, 1), :],
                                     sem_g.at[sl])

    def load_idx(blk_id, sl):
        cp = pltpu.make_async_copy(rt_hbm.at[blk_id], idx_smem.at[sl], sem_i)
        cp.start()
        cp.wait()

    def drain(sl):
        def body(r, c):
            row_copy(0, r, sl).wait()
            return c

        lax.fori_loop(0, blk, body, 0, unroll=8)

    @pl.when(b < n_used)
    def _():
        @pl.when((b == 0) & (f == 0))
        def _():
            load_idx(0, 0)

            def issue(r, c):
                row_copy(idx_smem[0, r], r, 0).start()
                return c

            lax.fori_loop(0, blk, issue, 0, unroll=8)

        @pl.when(f == 0)
        def _():
            drain(slot)
            xb_ref[...] = xf_ref[slot].astype(BF16)
            load_idx(jnp.where(last, b, b + 1), 1 - slot)

        base = f * rows_per_step
        for r in range(rows_per_step):
            row_copy(idx_smem[1 - slot, base + r], base + r, 1 - slot).start()

        @pl.when(last & (f == nf - 1))
        def _():
            drain(1 - slot)

        x = xb_ref[...]
        glu = jnp.minimum(_dot(x, w1g_ref[...]) + b1g_ref[...], SWIGLU_LIMIT)
        lin = jnp.clip(_dot(x, w1l_ref[...]) + b1l_ref[...], -SWIGLU_LIMIT, SWIGLU_LIMIT)
        act = glu * _sigmoid(SWIGLU_ALPHA * glu) * (lin + 1.0)
        contrib = _dot(act.astype(BF16), w2_ref[...])

        @pl.when(f == 0)
        def _():
            y_ref[...] = contrib

        @pl.when(f > 0)
        def _():
            y_ref[...] += contrib

        @pl.when(f == nf - 1)
        def _():
            y_ref[...] = (y_ref[...] + b2_ref[...]) * rg_ref[...]

    @pl.when((b >= n_used) & (f == 0))
    def _():
        y_ref[...] = jnp.zeros(y_ref.shape, F32)


def _moe_experts(block_e, n_used, row_tok2, h2, row_g, w1g, w1l, b1g, b1l, w2, b2, tf):
    n_blocks, blk = row_tok2.shape
    E, D, F = w1g.shape
    nf = F // tf

    def bb(b, nu):
        return jnp.minimum(b, nu[0] - 1)

    def ff(b, f, nu):
        return jnp.where(b < nu[0], f, nf - 1)

    grid_spec = pltpu.PrefetchScalarGridSpec(
        num_scalar_prefetch=2,
        grid=(n_blocks, nf),
        in_specs=[
            pl.BlockSpec(memory_space=pl.ANY),
            pl.BlockSpec(memory_space=pl.ANY),
            pl.BlockSpec((blk, 1), lambda b, f, be, nu: (bb(b, nu), 0)),
            pl.BlockSpec((None, D, tf), lambda b, f, be, nu: (be[bb(b, nu)], 0, ff(b, f, nu))),
            pl.BlockSpec((None, D, tf), lambda b, f, be, nu: (be[bb(b, nu)], 0, ff(b, f, nu))),
            pl.BlockSpec((None, 1, tf), lambda b, f, be, nu: (be[bb(b, nu)], 0, ff(b, f, nu))),
            pl.BlockSpec((None, 1, tf), lambda b, f, be, nu: (be[bb(b, nu)], 0, ff(b, f, nu))),
            pl.BlockSpec((None, tf, D), lambda b, f, be, nu: (be[bb(b, nu)], ff(b, f, nu), 0)),
            pl.BlockSpec((None, 1, D), lambda b, f, be, nu: (be[bb(b, nu)], 0, 0)),
        ],
        out_specs=pl.BlockSpec((blk, D), lambda b, f, be, nu: (b, 0)),
        scratch_shapes=[
            pltpu.SMEM((2, blk), jnp.int32),
            pltpu.VMEM((2, blk, D), F32),
            pltpu.VMEM((blk, D), BF16),
            pltpu.SemaphoreType.DMA,
            pltpu.SemaphoreType.DMA((2,)),
        ],
    )
    return pl.pallas_call(
        functools.partial(_moe_kernel, nf=nf),
        grid_spec=grid_spec,
        out_shape=jax.ShapeDtypeStruct((n_blocks * blk, D), F32),
        compiler_params=_cparams(2),
    )(block_e, n_used, row_tok2, h2, row_g, w1g, w1l, b1g, b1l, w2, b2)


def _combine_kernel(pos_hbm, ys_hbm, x1_ref, o_ref, idx_smem, buf_ref, sem_i, sem_g):
    i = pl.program_id(0)
    tc = x1_ref.shape[0]
    n = idx_smem.shape[1]
    slot = i % 2

    def row_copy(src, r, sl):
        return pltpu.make_async_copy(ys_hbm.at[pl.ds(src, 1), :], buf_ref.at[sl, pl.ds(r, 1), :],
                                     sem_g.at[sl])

    def start_gather(step, sl):
        cp = pltpu.make_async_copy(pos_hbm.at[step], idx_smem.at[sl], sem_i)
        cp.start()
        cp.wait()

        def issue(r, c):
            row_copy(idx_smem[sl, r], r, sl).start()
            return c

        lax.fori_loop(0, n, issue, 0, unroll=8)

    @pl.when(i == 0)
    def _():
        start_gather(0, 0)

    @pl.when(i + 1 < pl.num_programs(0))
    def _():
        start_gather(i + 1, 1 - slot)

    def drain(r, c):
        row_copy(0, r, slot).wait()
        return c

    lax.fori_loop(0, n, drain, 0, unroll=8)
    acc = x1_ref[...]
    for k in range(TOP_K):
        acc += buf_ref[slot, k * tc:(k + 1) * tc, :]
    o_ref[...] = acc


def _combine(pos_tiles, ys, x1, tc):
    T, D = x1.shape
    return pl.pallas_call(
        _combine_kernel,
        grid=(T // tc,),
        in_specs=[
            pl.BlockSpec(memory_space=pl.ANY),
            pl.BlockSpec(memory_space=pl.ANY),
            pl.BlockSpec((tc, D), lambda i: (i, 0)),
        ],
        out_specs=pl.BlockSpec((tc, D), lambda i: (i, 0)),
        out_shape=jax.ShapeDtypeStruct((T, D), F32),
        scratch_shapes=[
            pltpu.SMEM((2, TOP_K * tc), jnp.int32),
            pltpu.VMEM((2, TOP_K * tc, D), F32),
            pltpu.SemaphoreType.DMA,
            pltpu.SemaphoreType.DMA((2,)),
        ],
        compiler_params=_cparams(1),
    )(pos_tiles, ys, x1)


def _route(top_idx, top_gate, n_experts, blk):
    T = top_idx.shape[0]
    N = T * TOP_K
    e_flat = top_idx.reshape(N)
    g_flat = top_gate.reshape(N)
    tok_flat = jnp.repeat(jnp.arange(T, dtype=jnp.int32), TOP_K)
    onehot = (e_flat[:, None] == jnp.arange(n_experts, dtype=jnp.int32)[None, :]).astype(jnp.int32)
    csum = jnp.cumsum(onehot, axis=0)
    rank = jnp.sum(csum * onehot, axis=1) - 1
    counts = csum[-1]
    padded = ((counts + blk - 1) // blk) * blk
    pend = jnp.cumsum(padded)
    pstart = pend - padded
    dest = pstart[e_flat] + rank
    n_blocks = -(-(N + n_experts * (blk - 1)) // blk)
    rows = n_blocks * blk
    row_tok = jnp.zeros((rows,), jnp.int32).at[dest].set(tok_flat)
    row_g = jnp.zeros((rows,), F32).at[dest].set(g_flat)
    starts = jnp.arange(n_blocks, dtype=jnp.int32) * blk
    block_e = jnp.minimum(jnp.sum((pend[None, :] <= starts[:, None]).astype(jnp.int32), axis=1),
                          n_experts - 1).astype(jnp.int32)
    n_used = (pend[-1:] // blk).astype(jnp.int32)
    return dest, row_tok.reshape(n_blocks, blk), row_g.reshape(rows, 1), block_e, n_used


def _cmp_to_sel(nc, n_sel):
    ratio = SEL_BLOCK // CMP_STRIDE
    offs = (np.arange(ratio)[:, None] - np.arange(CMP_LEN // CMP_STRIDE)[None, :]).reshape(-1)
    diff = np.arange(nc)[:, None] - ratio * np.arange(n_sel)[None, :]
    return (diff[..., None] == offs).sum(-1).astype(np.float32)


def _pick(n, pref):
    for c in pref:
        if n % c == 0:
            return c
    return n


def kernel(x, mem, positions, mix_norm_g, mem_norm_g, w_in, w_out, gmlp_ws, gmlp_bs, gmlp_vg,
           nsa_cmp_pos, nsa_ck1, nsa_ck2, nsa_cv1, nsa_cv2, nsa_q_g, nsa_k_g,
           xattn_wkv, xattn_q_g, xattn_k_g, ffn_norm_g, w_router, b_router,
           w_e1, b_e1, w_e2, b_e2):
    B, S, D = x.shape
    T = B * S
    depth = w_in.shape[0]
    n_experts = w_router.shape[-1]
    NC = S // CMP_STRIDE
    n_sel = S // SEL_BLOCK
    tm = _pick(S, (512, 256, 128))
    tk = _pick(S, (512, 256, 128))

    half = HEAD_DIM // 2
    inv = ROPE_THETA ** (-jnp.arange(half, dtype=F32) * 2.0 / HEAD_DIM)
    ang = (positions.astype(F32).reshape(T, 1) * jnp.concatenate([inv, inv])[None, :])
    c2st = jnp.asarray(_cmp_to_sel(NC, n_sel).T, BF16)

    xs = x.reshape(T, D)
    for l in range(depth):
        wi = w_in[l]
        n_gate = 3 * NSA_REP
        zpad = jnp.zeros((D, 128 - n_gate), wi.dtype)
        w_r = jnp.concatenate(
            [wi[:, :3584], wi[:, 3608:4120],
             wi[:, 3584:3584 + n_gate], zpad, wi[:, 3584 + n_gate:3608], zpad], axis=1).astype(BF16)

        proj = _in_proj(xs, mix_norm_g[l][None, :], w_r, tm, PROJ_COLS // 2)

        bias_full = jnp.repeat(jnp.transpose(gmlp_bs[l]), HEAD_DIM, axis=1)
        y_a = _gmlp(proj, gmlp_ws[l], bias_full, gmlp_vg[l].reshape(1, GMLP_WIDTH), T, tm)

        qn, qr, ks, vs, kw, vw = _prep(proj, ang, nsa_q_g[l][None, :], nsa_k_g[l][None, :], B, S, tm)

        w1 = jnp.stack([nsa_ck1[l], nsa_cv1[l]]).astype(BF16)
        w2 = jnp.stack([nsa_ck2[l], nsa_cv2[l]]).astype(BF16)
        cmp_kv = _compress(proj.reshape(B, NC, CMP_STRIDE, PROJ_COLS), w1, w2,
                           nsa_cmp_pos[l], nsa_k_g[l][None, :], B, NC)

        o_c, sel = _cmp_attn(qn, cmp_kv, proj, c2st, B, S, Q_BLOCK)
        o_s = _sel_attn(qr, ks, vs, proj, sel, B, S, Q_BLOCK, tk)
        o_w = _win_attn(qr, kw, vw, proj, B, S, Q_BLOCK)

        km, vm = _mem_kv(mem, mem_norm_g[l][None, :], xattn_wkv[l].astype(BF16), xattn_k_g[l][None, :])
        y_c = _xattn(proj, km, vm, xattn_q_g[l][None, :], B, S, tm)

        wr = jnp.pad(w_router[l], ((0, 0), (0, 128 - n_experts)))
        wr_hi = wr.astype(BF16)
        wr_lo = (wr - wr_hi.astype(F32)).astype(BF16)
        br = jnp.pad(b_router[l], (0, 128 - n_experts))[None, :]
        x1, h2, ti, tg = _out_proj(xs, y_a, o_c, o_s, o_w, y_c, w_out[l].astype(BF16),
                                   ffn_norm_g[l][None, :], wr_hi, wr_lo, br, n_experts,
                                   _pick(S, (256, 128)))

        top_idx = ti[:, :TOP_K].astype(jnp.int32)
        dest, row_tok2, row_g, block_e, n_used = _route(top_idx, tg[:, :TOP_K], n_experts, MOE_BLOCK)

        d_ff = w_e2.shape[2]
        w1g, w1l = _deinterleave(w_e1[l].reshape(n_experts * D, 2 * d_ff), 512, _pick(2 * d_ff, (2048, 1024, 512, 256)))
        w1g = w1g.reshape(n_experts, D, d_ff)
        w1l = w1l.reshape(n_experts, D, d_ff)
        b1g = b_e1[l][:, None, 0::2]
        b1l = b_e1[l][:, None, 1::2]
        ys = _moe_experts(block_e, n_used, row_tok2, h2, row_g, w1g, w1l, b1g, b1l,
                          w_e2[l].astype(BF16), b_e2[l][:, None, :], 512)

        tc = 128
        pos_tiles = dest.reshape(T // tc, tc, TOP_K).transpose(0, 2, 1).reshape(T // tc, TOP_K * tc)
        xs = _combine(pos_tiles, ys, x1, tc)
    return xs.reshape(B, S, D)
```

```python
import functools

import numpy as np
import jax
import jax.numpy as jnp
from jax import lax
from jax.experimental import pallas as pl
from jax.experimental.pallas import tpu as pltpu

F32 = jnp.float32
BF16 = jnp.bfloat16

HEAD_DIM = 128
GMLP_GROUPS = 4
NSA_HEADS = 8
NSA_KV_HEADS = 2
NSA_REP = NSA_HEADS // NSA_KV_HEADS
XATTN_HEADS = 4
GMLP_WIDTH = GMLP_GROUPS * HEAD_DIM
NSA_WIDTH = NSA_HEADS * HEAD_DIM
XATTN_WIDTH = XATTN_HEADS * HEAD_DIM
GMLP_CHUNK = 128
CMP_LEN = 32
CMP_STRIDE = 16
SEL_BLOCK = 64
SEL_TOPK = 16
WINDOW = 512
Q_BLOCK = 128
TOP_K = 4
SWIGLU_ALPHA = 1.702
SWIGLU_LIMIT = 7.0
MOE_BLOCK = 512
ROPE_THETA = 10000.0
EPS = 1e-6
NEG_BIG = -1e30
MASK_BIG = 2.0 ** 100

COL_UV = 0
COL_Q = 8
COL_KV = 16
COL_QX = 28
COL_GATE = 32
N_COL_BLOCKS = 34
PROJ_COLS = N_COL_BLOCKS * 128

VMEM_LIMIT = 56 * 1024 * 1024


def _cparams(n_axes):
    return pltpu.CompilerParams(
        dimension_semantics=("arbitrary",) * n_axes, vmem_limit_bytes=VMEM_LIMIT)


def _rms(x, g):
    ms = jnp.mean(x * x, axis=-1, keepdims=True)
    return x * lax.rsqrt(ms + EPS) * g


def _gelu_tanh(x):
    c = np.float32(np.sqrt(2.0 / np.pi))
    return 0.5 * x * (1.0 + jnp.tanh(c * (x + 0.044715 * (x * x * x))))


def _sigmoid(x):
    return 1.0 / (1.0 + jnp.exp(-x))


def _dot(a, b):
    return jnp.dot(a, b, preferred_element_type=F32)


def _dot_nt(a, b):
    return lax.dot_general(a, b, (((1,), (1,)), ((), ())), preferred_element_type=F32)


def _inproj_kernel(x_ref, g_ref, w_ref, o_ref, h_ref):
    @pl.when(pl.program_id(1) == 0)
    def _():
        h_ref[...] = _rms(x_ref[...], g_ref[...]).astype(BF16)

    o_ref[...] = _dot(h_ref[...], w_ref[...])


def _in_proj(x2, g, w_r, tm, tn):
    T, D = x2.shape
    NC = w_r.shape[1]
    return pl.pallas_call(
        _inproj_kernel,
        grid=(T // tm, NC // tn),
        in_specs=[
            pl.BlockSpec((tm, D), lambda i, j: (i, 0)),
            pl.BlockSpec((1, D), lambda i, j: (0, 0)),
            pl.BlockSpec((D, tn), lambda i, j: (0, j)),
        ],
        out_specs=pl.BlockSpec((tm, tn), lambda i, j: (i, j)),
        out_shape=jax.ShapeDtypeStruct((T, NC), F32),
        scratch_shapes=[pltpu.VMEM((tm, D), BF16)],
        compiler_params=_cparams(2),
    )(x2, g, w_r)


def _prep_kernel(q_ref, sel_ref, win_ref, ang_ref, qg_ref, kg_ref,
                 qn_o, qr_o, ks_o, vs_o, kw_o, vw_o):
    ang = ang_ref[...]
    lane = lax.broadcasted_iota(jnp.int32, (1, HEAD_DIM), 1)
    cosf = jnp.cos(ang)
    sinf = jnp.sin(ang) * jnp.where(lane < HEAD_DIM // 2, -1.0, 1.0)
    scale = np.float32(HEAD_DIM ** -0.5 * np.log2(np.e))

    def rope(x):
        return x * cosf + pltpu.roll(x, HEAD_DIM // 2, axis=1) * sinf

    qg = qg_ref[...]
    kg = kg_ref[...]
    for h in range(NSA_HEADS):
        sl = slice(h * HEAD_DIM, (h + 1) * HEAD_DIM)
        n = _rms(q_ref[:, sl], qg)
        qn_o[:, sl] = (n * scale).astype(BF16)
        qr_o[:, sl] = (rope(n) * scale).astype(BF16)
    for g in range(NSA_KV_HEADS):
        sl = slice(g * HEAD_DIM, (g + 1) * HEAD_DIM)
        sv = slice((NSA_KV_HEADS + g) * HEAD_DIM, (NSA_KV_HEADS + g + 1) * HEAD_DIM)
        ks_o[g] = rope(_rms(sel_ref[:, sl], kg)).astype(BF16)
        vs_o[g] = sel_ref[:, sv].astype(BF16)
        kw_o[g] = rope(_rms(win_ref[:, sl], kg)).astype(BF16)
        vw_o[g] = win_ref[:, sv].astype(BF16)


def _prep(proj, ang, q_g, k_g, B, S, tm):
    T = B * S
    nsb = S // tm
    kv_shape = jax.ShapeDtypeStruct((B, NSA_KV_HEADS, S, HEAD_DIM), BF16)
    kv_spec = pl.BlockSpec((None, NSA_KV_HEADS, tm, HEAD_DIM),
                           lambda i: (i // nsb, 0, i % nsb, 0))
    return pl.pallas_call(
        _prep_kernel,
        grid=(T // tm,),
        in_specs=[
            pl.BlockSpec((tm, NSA_WIDTH), lambda i: (i, COL_Q * 128 // NSA_WIDTH)),
            pl.BlockSpec((tm, 512), lambda i: (i, (COL_KV + 4) * 128 // 512)),
            pl.BlockSpec((tm, 512), lambda i: (i, (COL_KV + 8) * 128 // 512)),
            pl.BlockSpec((tm, HEAD_DIM), lambda i: (i, 0)),
            pl.BlockSpec((1, HEAD_DIM), lambda i: (0, 0)),
            pl.BlockSpec((1, HEAD_DIM), lambda i: (0, 0)),
        ],
        out_specs=[
            pl.BlockSpec((tm, NSA_WIDTH), lambda i: (i, 0)),
            pl.BlockSpec((tm, NSA_WIDTH), lambda i: (i, 0)),
            kv_spec, kv_spec, kv_spec, kv_spec,
        ],
        out_shape=[
            jax.ShapeDtypeStruct((T, NSA_WIDTH), BF16),
            jax.ShapeDtypeStruct((T, NSA_WIDTH), BF16),
            kv_shape, kv_shape, kv_shape, kv_shape,
        ],
        compiler_params=_cparams(1),
    )(proj, proj, proj, ang, q_g, k_g)


def _gmlp_kernel(uv_ref, ws_ref, bias_ref, gv_ref, o_ref, *, n_chunks):
    row = lax.broadcasted_iota(jnp.int32, (GMLP_CHUNK, GMLP_CHUNK), 0)
    col = lax.broadcasted_iota(jnp.int32, (GMLP_CHUNK, GMLP_CHUNK), 1)
    causal = col <= row
    for g in range(GMLP_GROUPS):
        sl = slice(g * HEAD_DIM, (g + 1) * HEAD_DIM)
        sv = slice(GMLP_WIDTH + g * HEAD_DIM, GMLP_WIDTH + (g + 1) * HEAD_DIM)
        w = jnp.where(causal, ws_ref[g], 0.0).astype(BF16)
        u = _gelu_tanh(uv_ref[:, sl])
        v = _rms(_gelu_tanh(uv_ref[:, sv]), gv_ref[:, sl]).astype(BF16)
        bias = bias_ref[:, sl]
        for c in range(n_chunks):
            rs = slice(c * GMLP_CHUNK, (c + 1) * GMLP_CHUNK)
            vs = _dot(w, v[rs]) + bias
            o_ref[rs, sl] = (u[rs] * vs).astype(BF16)


def _gmlp(proj, ws, bias_full, gv, T, tm):
    return pl.pallas_call(
        functools.partial(_gmlp_kernel, n_chunks=tm // GMLP_CHUNK),
        grid=(T // tm,),
        in_specs=[
            pl.BlockSpec((tm, 2 * GMLP_WIDTH), lambda i: (i, 0)),
            pl.BlockSpec((GMLP_GROUPS, GMLP_CHUNK, GMLP_CHUNK), lambda i: (0, 0, 0)),
            pl.BlockSpec((GMLP_CHUNK, GMLP_WIDTH), lambda i: (0, 0)),
            pl.BlockSpec((1, GMLP_WIDTH), lambda i: (0, 0)),
        ],
        out_specs=pl.BlockSpec((tm, GMLP_WIDTH), lambda i: (i, 0)),
        out_shape=jax.ShapeDtypeStruct((T, GMLP_WIDTH), BF16),
        compiler_params=_cparams(1),
    )(proj, ws, bias_full, gv)


def _compress_kernel(x_ref, w1_ref, w2_ref, pos_ref, kg_ref, o_ref):
    nc = x_ref.shape[0]
    kind = pl.program_id(1)
    acc_a = jnp.zeros((nc, HEAD_DIM), F32)
    acc_b = jnp.zeros((nc, HEAD_DIM), F32)
    for j in range(CMP_STRIDE):
        xj = x_ref[:, j, :]
        wa = w1_ref[j * HEAD_DIM:(j + 1) * HEAD_DIM, :]
        wb = w1_ref[(CMP_STRIDE + j) * HEAD_DIM:(CMP_STRIDE + j + 1) * HEAD_DIM, :]
        acc_a += _dot((xj + pos_ref[j:j + 1, :]).astype(BF16), wa)
        acc_b += _dot((xj + pos_ref[CMP_STRIDE + j:CMP_STRIDE + j + 1, :]).astype(BF16), wb)
    hidden = acc_a + pltpu.roll(acc_b, nc - 1, axis=0)
    out = _dot(_gelu_tanh(hidden).astype(BF16), w2_ref[...])
    o_ref[...] = jnp.where(kind == 0, _rms(out, kg_ref[...]), out).astype(BF16)


def _compress(proj4, w1, w2, cmp_pos, k_g, B, NC):
    return pl.pallas_call(
        _compress_kernel,
        grid=(B, 2, NSA_KV_HEADS),
        in_specs=[
            pl.BlockSpec((None, NC, CMP_STRIDE, HEAD_DIM),
                         lambda b, k, g: (b, 0, 0, COL_KV + 2 * k + g)),
            pl.BlockSpec((None, CMP_LEN * HEAD_DIM, HEAD_DIM), lambda b, k, g: (k, 0, 0)),
            pl.BlockSpec((None, HEAD_DIM, HEAD_DIM), lambda b, k, g: (k, 0, 0)),
            pl.BlockSpec((CMP_LEN, HEAD_DIM), lambda b, k, g: (0, 0)),
            pl.BlockSpec((1, HEAD_DIM), lambda b, k, g: (0, 0)),
        ],
        out_specs=pl.BlockSpec((None, None, None, NC, HEAD_DIM), lambda b, k, g: (b, k, g, 0, 0)),
        out_shape=jax.ShapeDtypeStruct((B, 2, NSA_KV_HEADS, NC, HEAD_DIM), BF16),
        compiler_params=_cparams(3),
    )(proj4, w1, w2, cmp_pos, k_g)


def _cmp_attn_kernel(q_ref, kc_ref, vc_ref, gl_ref, c2st_ref, oc_ref, sel_ref, q_sc, *, k_top):
    i = pl.program_id(2)
    tq = q_ref.shape[0]
    nc = kc_ref.shape[0]
    n_sel = c2st_ref.shape[0]
    t = i * tq + lax.broadcasted_iota(jnp.int32, (tq, 1), 0)
    cmp_end = lax.broadcasted_iota(jnp.int32, (1, nc), 1) * CMP_STRIDE + (CMP_LEN - 1)
    bias = jnp.where(cmp_end <= t, 0.0, -jnp.inf)
    gl = gl_ref[...]
    for r in range(NSA_REP):
        q_sc[r * tq:(r + 1) * tq, :] = q_ref[:, r * HEAD_DIM:(r + 1) * HEAD_DIM]
    s = _dot_nt(q_sc[...], kc_ref[...]) + jnp.concatenate([bias] * NSA_REP, axis=0)
    m = jnp.max(s, axis=-1, keepdims=True)
    m = jnp.where(m > -jnp.inf, m, 0.0)
    e = jnp.exp2(s - m)
    d = jnp.sum(e, axis=-1, keepdims=True)
    p = e * (1.0 / jnp.where(d > 0, d, 1.0))
    o = _dot(p.astype(BF16), vc_ref[...])
    pc_sum = jnp.zeros((tq, nc), F32)
    for r in range(NSA_REP):
        rs = slice(r * tq, (r + 1) * tq)
        gate = _sigmoid(gl[:, 3 * r:3 * r + 1])
        oc_ref[:, r * HEAD_DIM:(r + 1) * HEAD_DIM] = o[rs, :] * gate
        pc_sum += p[rs, :]
    hi = pc_sum.astype(BF16)
    lo = (pc_sum - hi.astype(F32)).astype(BF16)
    imp = _dot_nt(c2st_ref[...], hi) + _dot_nt(c2st_ref[...], lo)

    sid = lax.broadcasted_iota(jnp.int32, (n_sel, 1), 0)
    sid_f = sid.astype(F32)
    tt = i * tq + lax.broadcasted_iota(jnp.int32, (1, tq), 1)
    cur = tt // SEL_BLOCK
    valid = sid * SEL_BLOCK <= tt
    forced = (sid == 0) | (sid == cur) | (sid == cur - 1)
    score = jnp.where(valid, jnp.where(forced, jnp.inf, imp), -jnp.inf)
    sel = jnp.zeros((n_sel, tq), F32)
    for _ in range(k_top):
        m = jnp.max(score, axis=0, keepdims=True)
        idx = jnp.min(jnp.where(score == m, sid_f, np.float32(n_sel)), axis=0, keepdims=True)
        pick = sid_f == idx
        sel = jnp.where(pick, 1.0, sel)
        score = jnp.where(pick, -jnp.inf, score)
    selm = jnp.transpose(sel - 1.0)
    pad = sel_ref.shape[1] - n_sel
    if pad:
        selm = jnp.concatenate([selm, jnp.full((tq, pad), -1.0, F32)], axis=1)
    sel_ref[...] = selm


def _cmp_attn(qn, cmp_kv, proj, c2st, B, S, tq):
    T = B * S
    nqb = S // tq
    NC = cmp_kv.shape[3]
    n_sel = S // SEL_BLOCK
    n_sel_pad = -(-n_sel // 128) * 128
    gw = NSA_REP * HEAD_DIM
    return pl.pallas_call(
        functools.partial(_cmp_attn_kernel, k_top=min(SEL_TOPK, n_sel)),
        grid=(B, NSA_KV_HEADS, nqb),
        in_specs=[
            pl.BlockSpec((tq, gw), lambda b, g, i: (b * nqb + i, g)),
            pl.BlockSpec((None, None, None, NC, HEAD_DIM), lambda b, g, i: (b, 0, g, 0, 0)),
            pl.BlockSpec((None, None, None, NC, HEAD_DIM), lambda b, g, i: (b, 1, g, 0, 0)),
            pl.BlockSpec((tq, 128), lambda b, g, i: (b * nqb + i, COL_GATE + g)),
            pl.BlockSpec((n_sel, NC), lambda b, g, i: (0, 0)),
        ],
        out_specs=[
            pl.BlockSpec((tq, gw), lambda b, g, i: (b * nqb + i, g)),
            pl.BlockSpec((None, None, tq, n_sel_pad), lambda b, g, i: (b, g, i, 0)),
        ],
        out_shape=[
            jax.ShapeDtypeStruct((T, NSA_WIDTH), F32),
            jax.ShapeDtypeStruct((B, NSA_KV_HEADS, S, n_sel_pad), F32),
        ],
        scratch_shapes=[pltpu.VMEM((NSA_REP * tq, HEAD_DIM), BF16)],
        compiler_params=_cparams(3),
    )(qn, cmp_kv, cmp_kv, proj, c2st)


def _sel_attn_kernel(q_ref, k_ref, v_ref, gl_ref, sel_ref, e0_ref, o_ref,
                     q_sc, m_sc, l_sc, acc_sc, *, tk, tkd):
    i = pl.program_id(2)
    tq = q_ref.shape[0]
    s0 = i * tq
    t = s0 + lax.broadcasted_iota(jnp.int32, (tq, 1), 0)
    for r in range(NSA_REP):
        q_sc[r * tq:(r + 1) * tq, :] = q_ref[:, r * HEAD_DIM:(r + 1) * HEAD_DIM]
    m_sc[...] = jnp.full(m_sc.shape, NEG_BIG, F32)
    l_sc[...] = jnp.zeros(l_sc.shape, F32)
    acc_sc[...] = jnp.zeros(acc_sc.shape, F32)
    def step(koff, width, diagonal):
        koff = pl.multiple_of(koff, width)
        k = k_ref[pl.ds(koff, width), :]
        v = v_ref[pl.ds(koff, width), :]
        c0 = koff // SEL_BLOCK
        slab = sel_ref[:, pl.ds(pl.multiple_of((c0 // 128) * 128, 128), 128)]
        rolled = pltpu.roll(slab, (128 - c0 % 128) % 128, axis=1)
        bias = _dot(rolled.astype(BF16), e0_ref[:, :width])
        if diagonal:
            key = koff + lax.broadcasted_iota(jnp.int32, (1, width), 1)
            bias = jnp.where(key <= t, bias, -MASK_BIG)
        s = _dot_nt(q_sc[...], k) + jnp.concatenate([bias] * NSA_REP, axis=0)
        m_old = m_sc[...]
        m_new = jnp.maximum(m_old, jnp.max(s, axis=-1, keepdims=True))
        alpha = jnp.exp2(m_old - m_new)
        p = jnp.exp2(s - jnp.concatenate([m_new] * (width // 128), axis=1))
        l_sc[...] = alpha * l_sc[...] + jnp.sum(p, axis=-1, keepdims=True)
        acc_sc[...] = alpha * acc_sc[...] + _dot(p.astype(BF16), v)
        m_sc[...] = m_new

    def body(j, carry):
        step(j * tk, tk, False)
        return carry

    n_wide = s0 // tk
    lax.fori_loop(0, n_wide, body, 0)
    tail = n_wide * tk
    n_narrow = (s0 - tail) // tkd
    for u in range(tk // tkd - 1):
        @pl.when(u < n_narrow)
        def _():
            step(tail + u * tkd, tkd, False)

    step(tail + n_narrow * tkd, tkd, True)
    gl = gl_ref[...]
    for r in range(NSA_REP):
        rs = slice(r * tq, (r + 1) * tq)
        gate = _sigmoid(gl[:, 3 * r + 1:3 * r + 2])
        o_ref[:, r * HEAD_DIM:(r + 1) * HEAD_DIM] = acc_sc[rs, :] / l_sc[rs, :] * gate


def _sel_attn(qr, ks, vs, proj, sel, B, S, tq, tk, tkd):
    assert tk % tkd == 0 and tkd % tq == 0 and tkd % SEL_BLOCK == 0 and tk // SEL_BLOCK <= 128
    T = B * S
    nqb = S // tq
    n_sel = S // SEL_BLOCK
    gw = NSA_REP * HEAD_DIM
    n_sel_pad = sel.shape[-1]
    e0 = np.where(np.arange(128)[:, None] == np.arange(tk)[None, :] // SEL_BLOCK, MASK_BIG, 0.0)
    return pl.pallas_call(
        functools.partial(_sel_attn_kernel, tk=tk, tkd=tkd),
        grid=(B, NSA_KV_HEADS, nqb),
        in_specs=[
            pl.BlockSpec((tq, gw), lambda b, g, i: (b * nqb + i, g)),
            pl.BlockSpec((None, None, S, HEAD_DIM), lambda b, g, i: (b, g, 0, 0)),
            pl.BlockSpec((None, None, S, HEAD_DIM), lambda b, g, i: (b, g, 0, 0)),
            pl.BlockSpec((tq, 128), lambda b, g, i: (b * nqb + i, COL_GATE + g)),
            pl.BlockSpec((None, None, tq, n_sel_pad), lambda b, g, i: (b, g, i, 0)),
            pl.BlockSpec((128, tk), lambda b, g, i: (0, 0)),
        ],
        out_specs=pl.BlockSpec((tq, gw), lambda b, g, i: (b * nqb + i, g)),
        out_shape=jax.ShapeDtypeStruct((T, NSA_WIDTH), F32),
        scratch_shapes=[
            pltpu.VMEM((NSA_REP * tq, HEAD_DIM), BF16),
            pltpu.VMEM((NSA_REP * tq, 128), F32),
            pltpu.VMEM((NSA_REP * tq, 128), F32),
            pltpu.VMEM((NSA_REP * tq, HEAD_DIM), F32),
        ],
        compiler_params=_cparams(3),
    )(qr, ks, vs, proj, sel, jnp.asarray(e0, BF16))


def _win_attn_kernel(q_ref, k_ref, v_ref, gl_ref, o_ref, q_sc, *, wk):
    i = pl.program_id(2)
    tq = q_ref.shape[0]
    s0 = i * tq
    t = s0 + lax.broadcasted_iota(jnp.int32, (tq, 1), 0)
    for r in range(NSA_REP):
        q_sc[r * tq:(r + 1) * tq, :] = q_ref[:, r * HEAD_DIM:(r + 1) * HEAD_DIM]
    ws = pl.multiple_of(jnp.maximum(s0 - WINDOW, 0), tq)
    k = k_ref[pl.ds(ws, wk), :]
    v = v_ref[pl.ds(ws, wk), :]
    kp = ws + lax.broadcasted_iota(jnp.int32, (1, wk), 1)
    bias = jnp.where(kp <= t, jnp.where(kp > t - WINDOW, 0.0, NEG_BIG), NEG_BIG)
    s = _dot_nt(q_sc[...], k) + jnp.concatenate([bias] * NSA_REP, axis=0)
    m = jnp.max(s, axis=-1, keepdims=True)
    e = jnp.exp2(s - m)
    p = e * (1.0 / jnp.sum(e, axis=-1, keepdims=True))
    o = _dot(p.astype(BF16), v)
    gl = gl_ref[...]
    for r in range(NSA_REP):
        gate = _sigmoid(gl[:, 3 * r + 2:3 * r + 3])
        o_ref[:, r * HEAD_DIM:(r + 1) * HEAD_DIM] = o[r * tq:(r + 1) * tq, :] * gate


def _win_attn(qr, kw, vw, proj, B, S, tq):
    T = B * S
    nqb = S // tq
    gw = NSA_REP * HEAD_DIM
    wk = min(WINDOW + tq, S)
    return pl.pallas_call(
        functools.partial(_win_attn_kernel, wk=wk),
        grid=(B, NSA_KV_HEADS, nqb),
        in_specs=[
            pl.BlockSpec((tq, gw), lambda b, g, i: (b * nqb + i, g)),
            pl.BlockSpec((None, None, S, HEAD_DIM), lambda b, g, i: (b, g, 0, 0)),
            pl.BlockSpec((None, None, S, HEAD_DIM), lambda b, g, i: (b, g, 0, 0)),
            pl.BlockSpec((tq, 128), lambda b, g, i: (b * nqb + i, COL_GATE + g)),
        ],
        out_specs=pl.BlockSpec((tq, gw), lambda b, g, i: (b * nqb + i, g)),
        out_shape=jax.ShapeDtypeStruct((T, NSA_WIDTH), F32),
        scratch_shapes=[pltpu.VMEM((NSA_REP * tq, HEAD_DIM), BF16)],
        compiler_params=_cparams(3),
    )(qr, kw, vw, proj)


def _memkv_kernel(mem_ref, g_ref, w_ref, kg_ref, k_o, v_o):
    h = _rms(mem_ref[...], g_ref[...]).astype(BF16)
    kv = _dot(h, w_ref[...])
    kg = kg_ref[...]
    for hd in range(XATTN_HEADS):
        sl = slice(hd * HEAD_DIM, (hd + 1) * HEAD_DIM)
        sv = slice(XATTN_WIDTH + hd * HEAD_DIM, XATTN_WIDTH + (hd + 1) * HEAD_DIM)
        k_o[:, sl] = _rms(kv[:, sl], kg).astype(BF16)
        v_o[:, sl] = kv[:, sv].astype(BF16)


def _mem_kv(mem, g, wkv, k_g):
    B, M, D = mem.shape
    shp = jax.ShapeDtypeStruct((B, M, XATTN_WIDTH), BF16)
    spec = pl.BlockSpec((None, M, XATTN_WIDTH), lambda b: (b, 0, 0))
    return pl.pallas_call(
        _memkv_kernel,
        grid=(B,),
        in_specs=[
            pl.BlockSpec((None, M, D), lambda b: (b, 0, 0)),
            pl.BlockSpec((1, D), lambda b: (0, 0)),
            pl.BlockSpec((D, 2 * XATTN_WIDTH), lambda b: (0, 0)),
            pl.BlockSpec((1, HEAD_DIM), lambda b: (0, 0)),
        ],
        out_specs=[spec, spec],
        out_shape=[shp, shp],
        compiler_params=_cparams(1),
    )(mem, g, wkv, k_g)


def _xattn_kernel(q_ref, k_ref, v_ref, qg_ref, o_ref):
    scale = np.float32(HEAD_DIM ** -0.5)
    qg = qg_ref[...]
    for hd in range(XATTN_HEADS):
        sl = slice(hd * HEAD_DIM, (hd + 1) * HEAD_DIM)
        q = (_rms(q_ref[:, sl], qg) * scale).astype(BF16)
        s = _dot_nt(q, k_ref[:, sl])
        m = jnp.max(s, axis=-1, keepdims=True)
        e = jnp.exp(s - m)
        p = e * (1.0 / jnp.sum(e, axis=-1, keepdims=True))
        o_ref[:, sl] = _dot(p.astype(BF16), v_ref[:, sl]).astype(BF16)


def _xattn(proj, km, vm, q_g, B, S, tm):
    T = B * S
    nsb = S // tm
    M = km.shape[1]
    return pl.pallas_call(
        _xattn_kernel,
        grid=(T // tm,),
        in_specs=[
            pl.BlockSpec((tm, XATTN_WIDTH), lambda i: (i, COL_QX * 128 // XATTN_WIDTH)),
            pl.BlockSpec((None, M, XATTN_WIDTH), lambda i: (i // nsb, 0, 0)),
            pl.BlockSpec((None, M, XATTN_WIDTH), lambda i: (i // nsb, 0, 0)),
            pl.BlockSpec((1, HEAD_DIM), lambda i: (0, 0)),
        ],
        out_specs=pl.BlockSpec((tm, XATTN_WIDTH), lambda i: (i, 0)),
        out_shape=jax.ShapeDtypeStruct((T, XATTN_WIDTH), BF16),
        compiler_params=_cparams(1),
    )(proj, km, vm, q_g)


def _outproj_kernel(x_ref, ya_ref, oc_ref, os_ref, ow_ref, yc_ref, wo_ref, g_ref,
                    wrh_ref, wrl_ref, br_ref, x1_o, h2_o, ti_o, tg_o, *, n_experts):
    yb = (oc_ref[...] + os_ref[...] + ow_ref[...]).astype(BF16)
    x1 = x_ref[...]
    x1 += _dot(ya_ref[...], wo_ref[0:GMLP_WIDTH, :])
    x1 += _dot(yb, wo_ref[GMLP_WIDTH:GMLP_WIDTH + NSA_WIDTH, :])
    x1 += _dot(yc_ref[...], wo_ref[GMLP_WIDTH + NSA_WIDTH:, :])
    x1_o[...] = x1
    h2 = _rms(x1, g_ref[...])
    h2_o[...] = h2
    hi = h2.astype(BF16)
    lo = (h2 - hi.astype(F32)).astype(BF16)
    logits = (_dot(hi, wrh_ref[...]) + _dot(lo, wrh_ref[...]) + _dot(hi, wrl_ref[...])
              + br_ref[...])
    lane = lax.broadcasted_iota(jnp.int32, (1, 128), 1)
    lane_f = lane.astype(F32)
    lg = jnp.where(lane < n_experts, logits, -jnp.inf)
    ti = jnp.zeros(logits.shape, F32)
    tg = jnp.zeros(logits.shape, F32)
    denom = jnp.zeros((logits.shape[0], 1), F32)
    v0 = None
    for k in range(TOP_K):
        m = jnp.max(lg, axis=-1, keepdims=True)
        ix = jnp.min(jnp.where(lg == m, lane_f, 128.0), axis=-1, keepdims=True)
        if v0 is None:
            v0 = m
        e = jnp.exp(m - v0)
        denom += e
        ti = jnp.where(lane == k, ix, ti)
        tg = jnp.where(lane == k, e, tg)
        lg = jnp.where(lane_f == ix, -jnp.inf, lg)
    ti_o[...] = ti
    tg_o[...] = tg / denom


def _out_proj(x2, ya, oc, os_, ow, yc, wo, g, wr_hi, wr_lo, br, n_experts, tm):
    T, D = x2.shape
    row = lambda w: pl.BlockSpec((tm, w), lambda i: (i, 0))
    full = lambda a: pl.BlockSpec(a.shape, lambda i: (0,) * a.ndim)
    return pl.pallas_call(
        functools.partial(_outproj_kernel, n_experts=n_experts),
        grid=(T // tm,),
        in_specs=[row(D), row(GMLP_WIDTH), row(NSA_WIDTH), row(NSA_WIDTH), row(NSA_WIDTH),
                  row(XATTN_WIDTH), full(wo), full(g), full(wr_hi), full(wr_lo), full(br)],
        out_specs=[row(D), row(D), row(128), row(128)],
        out_shape=[
            jax.ShapeDtypeStruct((T, D), F32),
            jax.ShapeDtypeStruct((T, D), F32),
            jax.ShapeDtypeStruct((T, 128), F32),
            jax.ShapeDtypeStruct((T, 128), F32),
        ],
        compiler_params=_cparams(1),
    )(x2, ya, oc, os_, ow, yc, wo, g, wr_hi, wr_lo, br)


def _deinterleave_kernel(w_ref, p_ref, g_o, l_o):
    perm = p_ref[...]
    for c in range(w_ref.shape[1] // 256):
        t = _dot(w_ref[:, c * 256:(c + 1) * 256].astype(BF16), perm)
        g_o[:, c * 128:(c + 1) * 128] = t[:, :128].astype(BF16)
        l_o[:, c * 128:(c + 1) * 128] = t[:, 128:].astype(BF16)


def _deinterleave(w, tr, tcw):
    R, C = w.shape
    src = np.arange(256)
    dst = np.where(src % 2 == 0, src // 2, 128 + src // 2)
    perm = np.zeros((256, 256), np.float32)
    perm[src, dst] = 1.0
    out = jax.ShapeDtypeStruct((R, C // 2), BF16)
    return pl.pallas_call(
        _deinterleave_kernel,
        grid=(R // tr, C // tcw),
        in_specs=[
            pl.BlockSpec((tr, tcw), lambda i, j: (i, j)),
            pl.BlockSpec((256, 256), lambda i, j: (0, 0)),
        ],
        out_specs=[pl.BlockSpec((tr, tcw // 2), lambda i, j: (i, j))] * 2,
        out_shape=[out, out],
        compiler_params=_cparams(2),
    )(w, jnp.asarray(perm, BF16))


def _moe_kernel(be_ref, nu_ref, rt_hbm, h2_hbm, w1g_ref, w1l_ref, b1g_ref, b1l_ref,
                w2_ref, b2_ref, y_ref, idx_smem, xf_ref, xb_ref, sem_i, sem_g, *, nf):
    b = pl.program_id(0)
    f = pl.program_id(1)
    blk = xb_ref.shape[0]
    rows_per_step = blk // nf
    n_used = nu_ref[0]
    slot = b % 2
    last = b == n_used - 1

    def row_copy(tok, r, sl):
        return pltpu.make_async_copy(h2_hbm.at[pl.ds(tok, 1), :], xf_ref.at[sl, pl.ds(r, 1), :],
                                     sem_g.at[sl])

    def load_idx(blk_id, sl):
        cp = pltpu.make_async_copy(rt_hbm.at[blk_id], idx_smem.at[sl], sem_i)
        cp.start()
        cp.wait()

    def drain(sl):
        def body(r, c):
            row_copy(0, r, sl).wait()
            return c

        lax.fori_loop(0, blk, body, 0, unroll=8)

    @pl.when(b < n_used)
    def _():
        @pl.when((b == 0) & (f == 0))
        def _():
            load_idx(0, 0)

            def issue(r, c):
                row_copy(idx_smem[0, r], r, 0).start()
                return c

            lax.fori_loop(0, blk, issue, 0, unroll=8)

        @pl.when(f == 0)
        def _():
            drain(slot)
            xb_ref[...] = xf_ref[slot].astype(BF16)
            load_idx(jnp.where(last, b, b + 1), 1 - slot)

        base = f * rows_per_step
        for r in range(rows_per_step):
            row_copy(idx_smem[1 - slot, base + r], base + r, 1 - slot).start()

        @pl.when(last & (f == nf - 1))
        def _():
            drain(1 - slot)

        x = xb_ref[...]
        glu = jnp.minimum(_dot(x, w1g_ref[...]) + b1g_ref[...], SWIGLU_LIMIT)
        lin = jnp.clip(_dot(x, w1l_ref[...]) + b1l_ref[...], -SWIGLU_LIMIT, SWIGLU_LIMIT)
        act = glu * _sigmoid(SWIGLU_ALPHA * glu) * (lin + 1.0)
        contrib = _dot(act.astype(BF16), w2_ref[...])

        @pl.when(f == 0)
        def _():
            y_ref[...] = contrib

        @pl.when(f > 0)
        def _():
            y_ref[...] += contrib

        @pl.when(f == nf - 1)
        def _():
            y_ref[...] = y_ref[...] + b2_ref[...]

    @pl.when((b >= n_used) & (f == 0))
    def _():
        y_ref[...] = jnp.zeros(y_ref.shape, F32)


def _moe_experts(block_e, n_used, row_tok2, h2, w1g, w1l, b1g, b1l, w2, b2, tf):
    n_blocks, blk = row_tok2.shape
    E, D, F = w1g.shape
    nf = F // tf

    def bb(b, nu):
        return jnp.minimum(b, nu[0] - 1)

    def ff(b, f, nu):
        return jnp.where(b < nu[0], f, nf - 1)

    grid_spec = pltpu.PrefetchScalarGridSpec(
        num_scalar_prefetch=2,
        grid=(n_blocks, nf),
        in_specs=[
            pl.BlockSpec(memory_space=pl.ANY),
            pl.BlockSpec(memory_space=pl.ANY),
            pl.BlockSpec((None, D, tf), lambda b, f, be, nu: (be[bb(b, nu)], 0, ff(b, f, nu))),
            pl.BlockSpec((None, D, tf), lambda b, f, be, nu: (be[bb(b, nu)], 0, ff(b, f, nu))),
            pl.BlockSpec((None, 1, tf), lambda b, f, be, nu: (be[bb(b, nu)], 0, ff(b, f, nu))),
            pl.BlockSpec((None, 1, tf), lambda b, f, be, nu: (be[bb(b, nu)], 0, ff(b, f, nu))),
            pl.BlockSpec((None, tf, D), lambda b, f, be, nu: (be[bb(b, nu)], ff(b, f, nu), 0)),
            pl.BlockSpec((None, 1, D), lambda b, f, be, nu: (be[bb(b, nu)], 0, 0)),
        ],
        out_specs=pl.BlockSpec((blk, D), lambda b, f, be, nu: (b, 0)),
        scratch_shapes=[
            pltpu.SMEM((2, blk), jnp.int32),
            pltpu.VMEM((2, blk, D), F32),
            pltpu.VMEM((blk, D), BF16),
            pltpu.SemaphoreType.DMA,
            pltpu.SemaphoreType.DMA((2,)),
        ],
    )
    return pl.pallas_call(
        functools.partial(_moe_kernel, nf=nf),
        grid_spec=grid_spec,
        out_shape=jax.ShapeDtypeStruct((n_blocks * blk, D), F32),
        compiler_params=_cparams(2),
    )(block_e, n_used, row_tok2, h2, w1g, w1l, b1g, b1l, w2, b2)


def _combine_kernel(pos_hbm, ys_hbm, x1_ref, tg_ref, o_ref, idx_smem, buf_ref, sem_i, sem_g):
    i = pl.program_id(0)
    tc = x1_ref.shape[0]
    n = idx_smem.shape[1]
    slot = i % 2

    def row_copy(src, r, sl):
        return pltpu.make_async_copy(ys_hbm.at[pl.ds(src, 1), :], buf_ref.at[sl, pl.ds(r, 1), :],
                                     sem_g.at[sl])

    def start_gather(step, sl):
        cp = pltpu.make_async_copy(pos_hbm.at[step], idx_smem.at[sl], sem_i)
        cp.start()
        cp.wait()

        def issue(r, c):
            row_copy(idx_smem[sl, r], r, sl).start()
            return c

        lax.fori_loop(0, n, issue, 0, unroll=8)

    @pl.when(i == 0)
    def _():
        start_gather(0, 0)

    @pl.when(i + 1 < pl.num_programs(0))
    def _():
        start_gather(i + 1, 1 - slot)

    def drain(r, c):
        row_copy(0, r, slot).wait()
        return c

    lax.fori_loop(0, n, drain, 0, unroll=8)
    acc = x1_ref[...]
    tg = tg_ref[...]
    for k in range(TOP_K):
        acc += buf_ref[slot, k * tc:(k + 1) * tc, :] * tg[:, k:k + 1]
    o_ref[...] = acc


def _combine(pos_tiles, ys, x1, tg, tc):
    T, D = x1.shape
    return pl.pallas_call(
        _combine_kernel,
        grid=(T // tc,),
        in_specs=[
            pl.BlockSpec(memory_space=pl.ANY),
            pl.BlockSpec(memory_space=pl.ANY),
            pl.BlockSpec((tc, D), lambda i: (i, 0)),
            pl.BlockSpec((tc, 128), lambda i: (i, 0)),
        ],
        out_specs=pl.BlockSpec((tc, D), lambda i: (i, 0)),
        out_shape=jax.ShapeDtypeStruct((T, D), F32),
        scratch_shapes=[
            pltpu.SMEM((2, TOP_K * tc), jnp.int32),
            pltpu.VMEM((2, TOP_K * tc, D), F32),
            pltpu.SemaphoreType.DMA,
            pltpu.SemaphoreType.DMA((2,)),
        ],
        compiler_params=_cparams(1),
    )(pos_tiles, ys, x1, tg)


def _route(top_idx, n_experts, blk):
    T = top_idx.shape[0]
    N = T * TOP_K
    e_flat = top_idx.reshape(N)
    tok_flat = jnp.repeat(jnp.arange(T, dtype=jnp.int32), TOP_K)
    onehot = (e_flat[:, None] == jnp.arange(n_experts, dtype=jnp.int32)[None, :]).astype(jnp.int32)
    csum = jnp.cumsum(onehot, axis=0)
    rank = jnp.sum(csum * onehot, axis=1) - 1
    counts = csum[-1]
    padded = ((counts + blk - 1) // blk) * blk
    pend = jnp.cumsum(padded)
    pstart = pend - padded
    dest = pstart[e_flat] + rank
    n_blocks = -(-(N + n_experts * (blk - 1)) // blk)
    rows = n_blocks * blk
    row_tok = jnp.zeros((rows,), jnp.int32).at[dest].set(tok_flat)
    starts = jnp.arange(n_blocks, dtype=jnp.int32) * blk
    block_e = jnp.minimum(jnp.sum((pend[None, :] <= starts[:, None]).astype(jnp.int32), axis=1),
                          n_experts - 1).astype(jnp.int32)
    n_used = (pend[-1:] // blk).astype(jnp.int32)
    return dest, row_tok.reshape(n_blocks, blk), block_e, n_used


def _cmp_to_sel(nc, n_sel):
    ratio = SEL_BLOCK // CMP_STRIDE
    offs = (np.arange(ratio)[:, None] - np.arange(CMP_LEN // CMP_STRIDE)[None, :]).reshape(-1)
    diff = np.arange(nc)[:, None] - ratio * np.arange(n_sel)[None, :]
    return (diff[..., None] == offs).sum(-1).astype(np.float32)


def _pick(n, pref):
    for c in pref:
        if n % c == 0:
            return c
    return n


def kernel(x, mem, positions, mix_norm_g, mem_norm_g, w_in, w_out, gmlp_ws, gmlp_bs, gmlp_vg,
           nsa_cmp_pos, nsa_ck1, nsa_ck2, nsa_cv1, nsa_cv2, nsa_q_g, nsa_k_g,
           xattn_wkv, xattn_q_g, xattn_k_g, ffn_norm_g, w_router, b_router,
           w_e1, b_e1, w_e2, b_e2):
    B, S, D = x.shape
    T = B * S
    depth = w_in.shape[0]
    n_experts = w_router.shape[-1]
    NC = S // CMP_STRIDE
    n_sel = S // SEL_BLOCK
    tm = _pick(S, (512, 256, 128))
    tk = _pick(S, (512, 256, 128))

    half = HEAD_DIM // 2
    inv = ROPE_THETA ** (-jnp.arange(half, dtype=F32) * 2.0 / HEAD_DIM)
    ang = (positions.astype(F32).reshape(T, 1) * jnp.concatenate([inv, inv])[None, :])
    c2st = jnp.asarray(_cmp_to_sel(NC, n_sel).T, BF16)

    xs = x.reshape(T, D)
    for l in range(depth):
        wi = w_in[l]
        n_gate = 3 * NSA_REP
        zpad = jnp.zeros((D, 128 - n_gate), wi.dtype)
        w_r = jnp.concatenate(
            [wi[:, :3584], wi[:, 3608:4120],
             wi[:, 3584:3584 + n_gate], zpad, wi[:, 3584 + n_gate:3608], zpad], axis=1).astype(BF16)

        proj = _in_proj(xs, mix_norm_g[l][None, :], w_r, tm, PROJ_COLS // 2)

        bias_full = jnp.repeat(jnp.transpose(gmlp_bs[l]), HEAD_DIM, axis=1)
        y_a = _gmlp(proj, gmlp_ws[l], bias_full, gmlp_vg[l].reshape(1, GMLP_WIDTH), T, tm)

        qn, qr, ks, vs, kw, vw = _prep(proj, ang, nsa_q_g[l][None, :], nsa_k_g[l][None, :], B, S, tm)

        w1 = jnp.stack([nsa_ck1[l], nsa_cv1[l]]).astype(BF16)
        w2 = jnp.stack([nsa_ck2[l], nsa_cv2[l]]).astype(BF16)
        cmp_kv = _compress(proj.reshape(B, NC, CMP_STRIDE, PROJ_COLS), w1, w2,
                           nsa_cmp_pos[l], nsa_k_g[l][None, :], B, NC)

        o_c, sel = _cmp_attn(qn, cmp_kv, proj, c2st, B, S, Q_BLOCK)
        o_s = _sel_attn(qr, ks, vs, proj, sel, B, S, _pick(S, (256, 128)), _pick(S, (1024, 512, 256)),
                        _pick(S, (512, 256)))
        o_w = _win_attn(qr, kw, vw, proj, B, S, Q_BLOCK)

        km, vm = _mem_kv(mem, mem_norm_g[l][None, :], xattn_wkv[l].astype(BF16), xattn_k_g[l][None, :])
        y_c = _xattn(proj, km, vm, xattn_q_g[l][None, :], B, S, tm)

        wr = jnp.pad(w_router[l], ((0, 0), (0, 128 - n_experts)))
        wr_hi = wr.astype(BF16)
        wr_lo = (wr - wr_hi.astype(F32)).astype(BF16)
        br = jnp.pad(b_router[l], (0, 128 - n_experts))[None, :]
        x1, h2, ti, tg = _out_proj(xs, y_a, o_c, o_s, o_w, y_c, w_out[l].astype(BF16),
                                   ffn_norm_g[l][None, :], wr_hi, wr_lo, br, n_experts,
                                   _pick(S, (256, 128)))

        top_idx = ti[:, :TOP_K].astype(jnp.int32)
        dest, row_tok2, block_e, n_used = _route(top_idx, n_experts, MOE_BLOCK)

        d_ff = w_e2.shape[2]
        w1g, w1l = _deinterleave(w_e1[l].reshape(n_experts * D, 2 * d_ff), 512, _pick(2 * d_ff, (2048, 1024, 512, 256)))
        w1g = w1g.reshape(n_experts, D, d_ff)
        w1l = w1l.reshape(n_experts, D, d_ff)
        b1g = b_e1[l][:, None, 0::2]
        b1l = b_e1[l][:, None, 1::2]
        ys = _moe_experts(block_e, n_used, row_tok2, h2, w1g, w1l, b1g, b1l,
                          w_e2[l].astype(BF16), b_e2[l][:, None, :], _pick(d_ff, (1024, 512)))

        tc = 128
        pos_tiles = dest.reshape(T // tc, tc, TOP_K).transpose(0, 2, 1).reshape(T // tc, TOP_K * tc)
        xs = _combine(pos_tiles, ys, x1, tg, tc)
    return xs.reshape(B, S, D)
```

```python
import functools

import numpy as np
import jax
import jax.numpy as jnp
from jax import lax
from jax.experimental import pallas as pl
from jax.experimental.pallas import tpu as pltpu

F32 = jnp.float32
BF16 = jnp.bfloat16

HEAD_DIM = 128
GMLP_GROUPS = 4
NSA_HEADS = 8
NSA_KV_HEADS = 2
NSA_REP = NSA_HEADS // NSA_KV_HEADS
XATTN_HEADS = 4
GMLP_WIDTH = GMLP_GROUPS * HEAD_DIM
NSA_WIDTH = NSA_HEADS * HEAD_DIM
XATTN_WIDTH = XATTN_HEADS * HEAD_DIM
GMLP_CHUNK = 128
CMP_LEN = 32
CMP_STRIDE = 16
SEL_BLOCK = 64
SEL_TOPK = 16
WINDOW = 512
Q_BLOCK = 128
TOP_K = 4
SWIGLU_ALPHA = 1.702
SWIGLU_LIMIT = 7.0
MOE_BLOCK = 512
ROPE_THETA = 10000.0
EPS = 1e-6
NEG_BIG = -1e30
MASK_BIG = 2.0 ** 100

COL_UV = 0
COL_Q = 8
COL_KV = 16
COL_QX = 28
COL_GATE = 32
N_COL_BLOCKS = 34
PROJ_COLS = N_COL_BLOCKS * 128

VMEM_LIMIT = 56 * 1024 * 1024


def _cparams(n_axes):
    return pltpu.CompilerParams(
        dimension_semantics=("arbitrary",) * n_axes, vmem_limit_bytes=VMEM_LIMIT)


def _rms(x, g):
    ms = jnp.mean(x * x, axis=-1, keepdims=True)
    return x * lax.rsqrt(ms + EPS) * g


def _gelu_tanh(x):
    c = np.float32(np.sqrt(2.0 / np.pi))
    return 0.5 * x * (1.0 + jnp.tanh(c * (x + 0.044715 * (x * x * x))))


def _sigmoid(x):
    return 1.0 / (1.0 + jnp.exp(-x))


def _dot(a, b):
    return jnp.dot(a, b, preferred_element_type=F32)


def _dot_nt(a, b):
    return lax.dot_general(a, b, (((1,), (1,)), ((), ())), preferred_element_type=F32)


def _inproj_kernel(x_ref, g_ref, w_ref, o_ref, h_ref):
    @pl.when(pl.program_id(1) == 0)
    def _():
        h_ref[...] = _rms(x_ref[...], g_ref[...]).astype(BF16)

    o_ref[...] = _dot(h_ref[...], w_ref[...])


def _in_proj(x2, g, w_r, tm, tn):
    T, D = x2.shape
    NC = w_r.shape[1]
    return pl.pallas_call(
        _inproj_kernel,
        grid=(T // tm, NC // tn),
        in_specs=[
            pl.BlockSpec((tm, D), lambda i, j: (i, 0)),
            pl.BlockSpec((1, D), lambda i, j: (0, 0)),
            pl.BlockSpec((D, tn), lambda i, j: (0, j)),
        ],
        out_specs=pl.BlockSpec((tm, tn), lambda i, j: (i, j)),
        out_shape=jax.ShapeDtypeStruct((T, NC), F32),
        scratch_shapes=[pltpu.VMEM((tm, D), BF16)],
        compiler_params=_cparams(2),
    )(x2, g, w_r)


def _prep_kernel(q_ref, sel_ref, win_ref, ang_ref, qg_ref, kg_ref,
                 qn_o, qr_o, ks_o, vs_o, kw_o, vw_o):
    ang = ang_ref[...]
    lane = lax.broadcasted_iota(jnp.int32, (1, HEAD_DIM), 1)
    cosf = jnp.cos(ang)
    sinf = jnp.sin(ang) * jnp.where(lane < HEAD_DIM // 2, -1.0, 1.0)
    scale = np.float32(HEAD_DIM ** -0.5 * np.log2(np.e))

    def rope(x):
        return x * cosf + pltpu.roll(x, HEAD_DIM // 2, axis=1) * sinf

    qg = qg_ref[...]
    kg = kg_ref[...]
    for h in range(NSA_HEADS):
        sl = slice(h * HEAD_DIM, (h + 1) * HEAD_DIM)
        n = _rms(q_ref[:, sl], qg)
        qn_o[:, sl] = (n * scale).astype(BF16)
        qr_o[:, sl] = (rope(n) * scale).astype(BF16)
    for g in range(NSA_KV_HEADS):
        sl = slice(g * HEAD_DIM, (g + 1) * HEAD_DIM)
        sv = slice((NSA_KV_HEADS + g) * HEAD_DIM, (NSA_KV_HEADS + g + 1) * HEAD_DIM)
        ks_o[g] = rope(_rms(sel_ref[:, sl], kg)).astype(BF16)
        vs_o[g] = sel_ref[:, sv].astype(BF16)
        kw_o[g] = rope(_rms(win_ref[:, sl], kg)).astype(BF16)
        vw_o[g] = win_ref[:, sv].astype(BF16)


def _prep(proj, ang, q_g, k_g, B, S, tm):
    T = B * S
    nsb = S // tm
    kv_shape = jax.ShapeDtypeStruct((B, NSA_KV_HEADS, S, HEAD_DIM), BF16)
    kv_spec = pl.BlockSpec((None, NSA_KV_HEADS, tm, HEAD_DIM),
                           lambda i: (i // nsb, 0, i % nsb, 0))
    return pl.pallas_call(
        _prep_kernel,
        grid=(T // tm,),
        in_specs=[
            pl.BlockSpec((tm, NSA_WIDTH), lambda i: (i, COL_Q * 128 // NSA_WIDTH)),
            pl.BlockSpec((tm, 512), lambda i: (i, (COL_KV + 4) * 128 // 512)),
            pl.BlockSpec((tm, 512), lambda i: (i, (COL_KV + 8) * 128 // 512)),
            pl.BlockSpec((tm, HEAD_DIM), lambda i: (i, 0)),
            pl.BlockSpec((1, HEAD_DIM), lambda i: (0, 0)),
            pl.BlockSpec((1, HEAD_DIM), lambda i: (0, 0)),
        ],
        out_specs=[
            pl.BlockSpec((tm, NSA_WIDTH), lambda i: (i, 0)),
            pl.BlockSpec((tm, NSA_WIDTH), lambda i: (i, 0)),
            kv_spec, kv_spec, kv_spec, kv_spec,
        ],
        out_shape=[
            jax.ShapeDtypeStruct((T, NSA_WIDTH), BF16),
            jax.ShapeDtypeStruct((T, NSA_WIDTH), BF16),
            kv_shape, kv_shape, kv_shape, kv_shape,
        ],
        compiler_params=_cparams(1),
    )(proj, proj, proj, ang, q_g, k_g)


def _gmlp_kernel(uv_ref, ws_ref, bias_ref, gv_ref, o_ref, *, n_chunks):
    row = lax.broadcasted_iota(jnp.int32, (GMLP_CHUNK, GMLP_CHUNK), 0)
    col = lax.broadcasted_iota(jnp.int32, (GMLP_CHUNK, GMLP_CHUNK), 1)
    causal = col <= row
    for g in range(GMLP_GROUPS):
        sl = slice(g * HEAD_DIM, (g + 1) * HEAD_DIM)
        sv = slice(GMLP_WIDTH + g * HEAD_DIM, GMLP_WIDTH + (g + 1) * HEAD_DIM)
        w = jnp.where(causal, ws_ref[g], 0.0).astype(BF16)
        u = _gelu_tanh(uv_ref[:, sl])
        v = _rms(_gelu_tanh(uv_ref[:, sv]), gv_ref[:, sl]).astype(BF16)
        bias = bias_ref[:, sl]
        for c in range(n_chunks):
            rs = slice(c * GMLP_CHUNK, (c + 1) * GMLP_CHUNK)
            vs = _dot(w, v[rs]) + bias
            o_ref[rs, sl] = (u[rs] * vs).astype(BF16)


def _gmlp(proj, ws, bias_full, gv, T, tm):
    return pl.pallas_call(
        functools.partial(_gmlp_kernel, n_chunks=tm // GMLP_CHUNK),
        grid=(T // tm,),
        in_specs=[
            pl.BlockSpec((tm, 2 * GMLP_WIDTH), lambda i: (i, 0)),
            pl.BlockSpec((GMLP_GROUPS, GMLP_CHUNK, GMLP_CHUNK), lambda i: (0, 0, 0)),
            pl.BlockSpec((GMLP_CHUNK, GMLP_WIDTH), lambda i: (0, 0)),
            pl.BlockSpec((1, GMLP_WIDTH), lambda i: (0, 0)),
        ],
        out_specs=pl.BlockSpec((tm, GMLP_WIDTH), lambda i: (i, 0)),
        out_shape=jax.ShapeDtypeStruct((T, GMLP_WIDTH), BF16),
        compiler_params=_cparams(1),
    )(proj, ws, bias_full, gv)


def _compress_kernel(x_ref, w1_ref, w2_ref, pos_ref, kg_ref, o_ref):
    nc = x_ref.shape[0]
    kind = pl.program_id(1)
    acc_a = jnp.zeros((nc, HEAD_DIM), F32)
    acc_b = jnp.zeros((nc, HEAD_DIM), F32)
    for j in range(CMP_STRIDE):
        xj = x_ref[:, j, :]
        wa = w1_ref[j * HEAD_DIM:(j + 1) * HEAD_DIM, :]
        wb = w1_ref[(CMP_STRIDE + j) * HEAD_DIM:(CMP_STRIDE + j + 1) * HEAD_DIM, :]
        acc_a += _dot((xj + pos_ref[j:j + 1, :]).astype(BF16), wa)
        acc_b += _dot((xj + pos_ref[CMP_STRIDE + j:CMP_STRIDE + j + 1, :]).astype(BF16), wb)
    hidden = acc_a + pltpu.roll(acc_b, nc - 1, axis=0)
    out = _dot(_gelu_tanh(hidden).astype(BF16), w2_ref[...])
    o_ref[...] = jnp.where(kind == 0, _rms(out, kg_ref[...]), out).astype(BF16)


def _compress(proj4, w1, w2, cmp_pos, k_g, B, NC):
    return pl.pallas_call(
        _compress_kernel,
        grid=(B, 2, NSA_KV_HEADS),
        in_specs=[
            pl.BlockSpec((None, NC, CMP_STRIDE, HEAD_DIM),
                         lambda b, k, g: (b, 0, 0, COL_KV + 2 * k + g)),
            pl.BlockSpec((None, CMP_LEN * HEAD_DIM, HEAD_DIM), lambda b, k, g: (k, 0, 0)),
            pl.BlockSpec((None, HEAD_DIM, HEAD_DIM), lambda b, k, g: (k, 0, 0)),
            pl.BlockSpec((CMP_LEN, HEAD_DIM), lambda b, k, g: (0, 0)),
            pl.BlockSpec((1, HEAD_DIM), lambda b, k, g: (0, 0)),
        ],
        out_specs=pl.BlockSpec((None, None, None, NC, HEAD_DIM), lambda b, k, g: (b, k, g, 0, 0)),
        out_shape=jax.ShapeDtypeStruct((B, 2, NSA_KV_HEADS, NC, HEAD_DIM), BF16),
        compiler_params=_cparams(3),
    )(proj4, w1, w2, cmp_pos, k_g)


def _cmp_attn_kernel(q_ref, kc_ref, vc_ref, gl_ref, c2st_ref, oc_ref, sel_ref, q_sc, imp_sc,
                     *, k_top, widths):
    i = pl.program_id(2)
    tq = q_ref.shape[0]
    n_sel = c2st_ref.shape[0]
    t = i * tq + lax.broadcasted_iota(jnp.int32, (tq, 1), 0)
    gl = gl_ref[...]
    for r in range(NSA_REP):
        q_sc[r * tq:(r + 1) * tq, :] = q_ref[:, r * HEAD_DIM:(r + 1) * HEAD_DIM]

    def attend(nc):
        cmp_end = lax.broadcasted_iota(jnp.int32, (1, nc), 1) * CMP_STRIDE + (CMP_LEN - 1)
        bias = jnp.where(cmp_end <= t, 0.0, -jnp.inf)
        s = _dot_nt(q_sc[...], kc_ref[0:nc, :]) + jnp.concatenate([bias] * NSA_REP, axis=0)
        m = jnp.max(s, axis=-1, keepdims=True)
        m = jnp.where(m > -jnp.inf, m, 0.0)
        e = jnp.exp2(s - m)
        d = jnp.sum(e, axis=-1, keepdims=True)
        p = e * (1.0 / jnp.where(d > 0, d, 1.0))
        o = _dot(p.astype(BF16), vc_ref[0:nc, :])
        pc_sum = jnp.zeros((tq, nc), F32)
        for r in range(NSA_REP):
            rs = slice(r * tq, (r + 1) * tq)
            gate = _sigmoid(gl[:, 3 * r:3 * r + 1])
            oc_ref[:, r * HEAD_DIM:(r + 1) * HEAD_DIM] = o[rs, :] * gate
            pc_sum += p[rs, :]
        hi = pc_sum.astype(BF16)
        lo = (pc_sum - hi.astype(F32)).astype(BF16)
        imp_sc[...] = _dot_nt(c2st_ref[:, 0:nc], hi) + _dot_nt(c2st_ref[:, 0:nc], lo)

    needed = (i * tq + tq - CMP_LEN) // CMP_STRIDE + 1
    for j, w in enumerate(widths):
        conds = ([needed <= w] if j + 1 < len(widths) else []) + ([needed > widths[j - 1]] if j else [])
        if conds:
            pl.when(functools.reduce(jnp.logical_and, conds))(functools.partial(attend, w))
        else:
            attend(w)

    imp = imp_sc[...]
    sid = lax.broadcasted_iota(jnp.int32, (n_sel, 1), 0)
    sid_f = sid.astype(F32)
    tt = i * tq + lax.broadcasted_iota(jnp.int32, (1, tq), 1)
    cur = tt // SEL_BLOCK
    valid = sid * SEL_BLOCK <= tt
    forced = (sid == 0) | (sid == cur) | (sid == cur - 1)
    score = jnp.where(valid, jnp.where(forced, jnp.inf, imp), -jnp.inf)
    sel = jnp.zeros((n_sel, tq), F32)
    for _ in range(k_top):
        m = jnp.max(score, axis=0, keepdims=True)
        idx = jnp.min(jnp.where(score == m, sid_f, np.float32(n_sel)), axis=0, keepdims=True)
        pick = sid_f == idx
        sel = jnp.where(pick, 1.0, sel)
        score = jnp.where(pick, -jnp.inf, score)
    selm = jnp.transpose(sel - 1.0)
    pad = sel_ref.shape[1] - n_sel
    if pad:
        selm = jnp.concatenate([selm, jnp.full((tq, pad), -1.0, F32)], axis=1)
    sel_ref[...] = selm


def _cmp_attn(qn, cmp_kv, proj, c2st, B, S, tq):
    T = B * S
    nqb = S // tq
    NC = cmp_kv.shape[3]
    n_sel = S // SEL_BLOCK
    n_sel_pad = -(-n_sel // 128) * 128
    gw = NSA_REP * HEAD_DIM
    return pl.pallas_call(
        functools.partial(_cmp_attn_kernel, k_top=min(SEL_TOPK, n_sel),
                          widths=tuple(NC * (j + 1) // 4 for j in range(4)) if NC % 1024 == 0 else (NC,)),
        grid=(B, NSA_KV_HEADS, nqb),
        in_specs=[
            pl.BlockSpec((tq, gw), lambda b, g, i: (b * nqb + i, g)),
            pl.BlockSpec((None, None, None, NC, HEAD_DIM), lambda b, g, i: (b, 0, g, 0, 0)),
            pl.BlockSpec((None, None, None, NC, HEAD_DIM), lambda b, g, i: (b, 1, g, 0, 0)),
            pl.BlockSpec((tq, 128), lambda b, g, i: (b * nqb + i, COL_GATE + g)),
            pl.BlockSpec((n_sel, NC), lambda b, g, i: (0, 0)),
        ],
        out_specs=[
            pl.BlockSpec((tq, gw), lambda b, g, i: (b * nqb + i, g)),
            pl.BlockSpec((None, None, tq, n_sel_pad), lambda b, g, i: (b, g, i, 0)),
        ],
        out_shape=[
            jax.ShapeDtypeStruct((T, NSA_WIDTH), F32),
            jax.ShapeDtypeStruct((B, NSA_KV_HEADS, S, n_sel_pad), F32),
        ],
        scratch_shapes=[pltpu.VMEM((NSA_REP * tq, HEAD_DIM), BF16), pltpu.VMEM((n_sel, tq), F32)],
        compiler_params=_cparams(3),
    )(qn, cmp_kv, cmp_kv, proj, c2st)


def _sel_attn_kernel(q_ref, k_ref, v_ref, gl_ref, sel_ref, e0_ref, o_ref,
                     q_sc, m_sc, l_sc, acc_sc, *, tk, tkd):
    i = pl.program_id(2)
    tq = q_ref.shape[0]
    s0 = i * tq
    t = s0 + lax.broadcasted_iota(jnp.int32, (tq, 1), 0)
    for r in range(NSA_REP):
        q_sc[r * tq:(r + 1) * tq, :] = q_ref[:, r * HEAD_DIM:(r + 1) * HEAD_DIM]
    m_sc[...] = jnp.full(m_sc.shape, NEG_BIG, F32)
    l_sc[...] = jnp.zeros(l_sc.shape, F32)
    acc_sc[...] = jnp.zeros(acc_sc.shape, F32)
    def step(koff, width, diagonal):
        koff = pl.multiple_of(koff, width)
        k = k_ref[pl.ds(koff, width), :]
        v = v_ref[pl.ds(koff, width), :]
        c0 = koff // SEL_BLOCK
        slab = sel_ref[:, pl.ds(pl.multiple_of((c0 // 128) * 128, 128), 128)]
        rolled = pltpu.roll(slab, (128 - c0 % 128) % 128, axis=1)
        bias = _dot(rolled.astype(BF16), e0_ref[:, :width])
        if diagonal:
            key = koff + lax.broadcasted_iota(jnp.int32, (1, width), 1)
            bias = jnp.where(key <= t, bias, -MASK_BIG)
        s = _dot_nt(q_sc[...], k) + jnp.concatenate([bias] * NSA_REP, axis=0)
        m_old = m_sc[...]
        m_new = jnp.maximum(m_old, jnp.max(s, axis=-1, keepdims=True))
        alpha = jnp.exp2(m_old - m_new)
        p = jnp.exp2(s - jnp.concatenate([m_new] * (width // 128), axis=1))
        l_sc[...] = alpha * l_sc[...] + jnp.sum(p, axis=-1, keepdims=True)
        acc_sc[...] = alpha * acc_sc[...] + _dot(p.astype(BF16), v)
        m_sc[...] = m_new

    def body(j, carry):
        step(j * tk, tk, False)
        return carry

    n_wide = s0 // tk
    lax.fori_loop(0, n_wide, body, 0)
    tail = n_wide * tk
    n_narrow = (s0 - tail) // tkd
    for u in range(tk // tkd - 1):
        @pl.when(u < n_narrow)
        def _():
            step(tail + u * tkd, tkd, False)

    step(tail + n_narrow * tkd, tkd, True)
    gl = gl_ref[...]
    for r in range(NSA_REP):
        rs = slice(r * tq, (r + 1) * tq)
        gate = _sigmoid(gl[:, 3 * r + 1:3 * r + 2])
        o_ref[:, r * HEAD_DIM:(r + 1) * HEAD_DIM] = acc_sc[rs, :] / l_sc[rs, :] * gate


def _sel_attn(qr, ks, vs, proj, sel, B, S, tq, tk, tkd):
    assert tk % tkd == 0 and tkd % tq == 0 and tkd % SEL_BLOCK == 0 and tk // SEL_BLOCK <= 128
    T = B * S
    nqb = S // tq
    n_sel = S // SEL_BLOCK
    gw = NSA_REP * HEAD_DIM
    n_sel_pad = sel.shape[-1]
    e0 = np.where(np.arange(128)[:, None] == np.arange(tk)[None, :] // SEL_BLOCK, MASK_BIG, 0.0)
    return pl.pallas_call(
        functools.partial(_sel_attn_kernel, tk=tk, tkd=tkd),
        grid=(B, NSA_KV_HEADS, nqb),
        in_specs=[
            pl.BlockSpec((tq, gw), lambda b, g, i: (b * nqb + i, g)),
            pl.BlockSpec((None, None, S, HEAD_DIM), lambda b, g, i: (b, g, 0, 0)),
            pl.BlockSpec((None, None, S, HEAD_DIM), lambda b, g, i: (b, g, 0, 0)),
            pl.BlockSpec((tq, 128), lambda b, g, i: (b * nqb + i, COL_GATE + g)),
            pl.BlockSpec((None, None, tq, n_sel_pad), lambda b, g, i: (b, g, i, 0)),
            pl.BlockSpec((128, tk), lambda b, g, i: (0, 0)),
        ],
        out_specs=pl.BlockSpec((tq, gw), lambda b, g, i: (b * nqb + i, g)),
        out_shape=jax.ShapeDtypeStruct((T, NSA_WIDTH), F32),
        scratch_shapes=[
            pltpu.VMEM((NSA_REP * tq, HEAD_DIM), BF16),
            pltpu.VMEM((NSA_REP * tq, 128), F32),
            pltpu.VMEM((NSA_REP * tq, 128), F32),
            pltpu.VMEM((NSA_REP * tq, HEAD_DIM), F32),
        ],
        compiler_params=_cparams(3),
    )(qr, ks, vs, proj, sel, jnp.asarray(e0, BF16))


def _win_attn_kernel(q_ref, k_ref, v_ref, gl_ref, o_ref, q_sc, *, wk):
    i = pl.program_id(2)
    tq = q_ref.shape[0]
    s0 = i * tq
    t = s0 + lax.broadcasted_iota(jnp.int32, (tq, 1), 0)
    for r in range(NSA_REP):
        q_sc[r * tq:(r + 1) * tq, :] = q_ref[:, r * HEAD_DIM:(r + 1) * HEAD_DIM]
    ws = pl.multiple_of(jnp.maximum(s0 - WINDOW, 0), tq)
    k = k_ref[pl.ds(ws, wk), :]
    v = v_ref[pl.ds(ws, wk), :]
    kp = ws + lax.broadcasted_iota(jnp.int32, (1, wk), 1)
    bias = jnp.where(kp <= t, jnp.where(kp > t - WINDOW, 0.0, NEG_BIG), NEG_BIG)
    s = _dot_nt(q_sc[...], k) + jnp.concatenate([bias] * NSA_REP, axis=0)
    m = jnp.max(s, axis=-1, keepdims=True)
    e = jnp.exp2(s - m)
    p = e * (1.0 / jnp.sum(e, axis=-1, keepdims=True))
    o = _dot(p.astype(BF16), v)
    gl = gl_ref[...]
    for r in range(NSA_REP):
        gate = _sigmoid(gl[:, 3 * r + 2:3 * r + 3])
        o_ref[:, r * HEAD_DIM:(r + 1) * HEAD_DIM] = o[r * tq:(r + 1) * tq, :] * gate


def _win_attn(qr, kw, vw, proj, B, S, tq):
    T = B * S
    nqb = S // tq
    gw = NSA_REP * HEAD_DIM
    wk = min(WINDOW + tq, S)
    return pl.pallas_call(
        functools.partial(_win_attn_kernel, wk=wk),
        grid=(B, NSA_KV_HEADS, nqb),
        in_specs=[
            pl.BlockSpec((tq, gw), lambda b, g, i: (b * nqb + i, g)),
            pl.BlockSpec((None, None, S, HEAD_DIM), lambda b, g, i: (b, g, 0, 0)),
            pl.BlockSpec((None, None, S, HEAD_DIM), lambda b, g, i: (b, g, 0, 0)),
            pl.BlockSpec((tq, 128), lambda b, g, i: (b * nqb + i, COL_GATE + g)),
        ],
        out_specs=pl.BlockSpec((tq, gw), lambda b, g, i: (b * nqb + i, g)),
        out_shape=jax.ShapeDtypeStruct((T, NSA_WIDTH), F32),
        scratch_shapes=[pltpu.VMEM((NSA_REP * tq, HEAD_DIM), BF16)],
        compiler_params=_cparams(3),
    )(qr, kw, vw, proj)


def _memkv_kernel(mem_ref, g_ref, w_ref, kg_ref, k_o, v_o):
    h = _rms(mem_ref[...], g_ref[...]).astype(BF16)
    kv = _dot(h, w_ref[...])
    kg = kg_ref[...]
    for hd in range(XATTN_HEADS):
        sl = slice(hd * HEAD_DIM, (hd + 1) * HEAD_DIM)
        sv = slice(XATTN_WIDTH + hd * HEAD_DIM, XATTN_WIDTH + (hd + 1) * HEAD_DIM)
        k_o[:, sl] = _rms(kv[:, sl], kg).astype(BF16)
        v_o[:, sl] = kv[:, sv].astype(BF16)


def _mem_kv(mem, g, wkv, k_g):
    B, M, D = mem.shape
    shp = jax.ShapeDtypeStruct((B, M, XATTN_WIDTH), BF16)
    spec = pl.BlockSpec((None, M, XATTN_WIDTH), lambda b: (b, 0, 0))
    return pl.pallas_call(
        _memkv_kernel,
        grid=(B,),
        in_specs=[
            pl.BlockSpec((None, M, D), lambda b: (b, 0, 0)),
            pl.BlockSpec((1, D), lambda b: (0, 0)),
            pl.BlockSpec((D, 2 * XATTN_WIDTH), lambda b: (0, 0)),
            pl.BlockSpec((1, HEAD_DIM), lambda b: (0, 0)),
        ],
        out_specs=[spec, spec],
        out_shape=[shp, shp],
        compiler_params=_cparams(1),
    )(mem, g, wkv, k_g)


def _xattn_kernel(q_ref, k_ref, v_ref, qg_ref, o_ref):
    scale = np.float32(HEAD_DIM ** -0.5)
    qg = qg_ref[...]
    for hd in range(XATTN_HEADS):
        sl = slice(hd * HEAD_DIM, (hd + 1) * HEAD_DIM)
        q = (_rms(q_ref[:, sl], qg) * scale).astype(BF16)
        s = _dot_nt(q, k_ref[:, sl])
        m = jnp.max(s, axis=-1, keepdims=True)
        e = jnp.exp(s - m)
        p = e * (1.0 / jnp.sum(e, axis=-1, keepdims=True))
        o_ref[:, sl] = _dot(p.astype(BF16), v_ref[:, sl]).astype(BF16)


def _xattn(proj, km, vm, q_g, B, S, tm):
    T = B * S
    nsb = S // tm
    M = km.shape[1]
    return pl.pallas_call(
        _xattn_kernel,
        grid=(T // tm,),
        in_specs=[
            pl.BlockSpec((tm, XATTN_WIDTH), lambda i: (i, COL_QX * 128 // XATTN_WIDTH)),
            pl.BlockSpec((None, M, XATTN_WIDTH), lambda i: (i // nsb, 0, 0)),
            pl.BlockSpec((None, M, XATTN_WIDTH), lambda i: (i // nsb, 0, 0)),
            pl.BlockSpec((1, HEAD_DIM), lambda i: (0, 0)),
        ],
        out_specs=pl.BlockSpec((tm, XATTN_WIDTH), lambda i: (i, 0)),
        out_shape=jax.ShapeDtypeStruct((T, XATTN_WIDTH), BF16),
        compiler_params=_cparams(1),
    )(proj, km, vm, q_g)


def _outproj_kernel(x_ref, ya_ref, oc_ref, os_ref, ow_ref, yc_ref, wo_ref, g_ref,
                    wrh_ref, wrl_ref, br_ref, x1_o, h2_o, ti_o, tg_o, *, n_experts):
    yb = (oc_ref[...] + os_ref[...] + ow_ref[...]).astype(BF16)
    x1 = x_ref[...]
    x1 += _dot(ya_ref[...], wo_ref[0:GMLP_WIDTH, :])
    x1 += _dot(yb, wo_ref[GMLP_WIDTH:GMLP_WIDTH + NSA_WIDTH, :])
    x1 += _dot(yc_ref[...], wo_ref[GMLP_WIDTH + NSA_WIDTH:, :])
    x1_o[...] = x1
    h2 = _rms(x1, g_ref[...])
    h2_o[...] = h2
    hi = h2.astype(BF16)
    lo = (h2 - hi.astype(F32)).astype(BF16)
    logits = (_dot(hi, wrh_ref[...]) + _dot(lo, wrh_ref[...]) + _dot(hi, wrl_ref[...])
              + br_ref[...])
    lane = lax.broadcasted_iota(jnp.int32, (1, 128), 1)
    lane_f = lane.astype(F32)
    lg = jnp.where(lane < n_experts, logits, -jnp.inf)
    ti = jnp.zeros(logits.shape, F32)
    tg = jnp.zeros(logits.shape, F32)
    denom = jnp.zeros((logits.shape[0], 1), F32)
    v0 = None
    for k in range(TOP_K):
        m = jnp.max(lg, axis=-1, keepdims=True)
        ix = jnp.min(jnp.where(lg == m, lane_f, 128.0), axis=-1, keepdims=True)
        if v0 is None:
            v0 = m
        e = jnp.exp(m - v0)
        denom += e
        ti = jnp.where(lane == k, ix, ti)
        tg = jnp.where(lane == k, e, tg)
        lg = jnp.where(lane_f == ix, -jnp.inf, lg)
    ti_o[...] = ti
    tg_o[...] = tg / denom


def _out_proj(x2, ya, oc, os_, ow, yc, wo, g, wr_hi, wr_lo, br, n_experts, tm):
    T, D = x2.shape
    row = lambda w: pl.BlockSpec((tm, w), lambda i: (i, 0))
    full = lambda a: pl.BlockSpec(a.shape, lambda i: (0,) * a.ndim)
    return pl.pallas_call(
        functools.partial(_outproj_kernel, n_experts=n_experts),
        grid=(T // tm,),
        in_specs=[row(D), row(GMLP_WIDTH), row(NSA_WIDTH), row(NSA_WIDTH), row(NSA_WIDTH),
                  row(XATTN_WIDTH), full(wo), full(g), full(wr_hi), full(wr_lo), full(br)],
        out_specs=[row(D), row(D), row(128), row(128)],
        out_shape=[
            jax.ShapeDtypeStruct((T, D), F32),
            jax.ShapeDtypeStruct((T, D), F32),
            jax.ShapeDtypeStruct((T, 128), F32),
            jax.ShapeDtypeStruct((T, 128), F32),
        ],
        compiler_params=_cparams(1),
    )(x2, ya, oc, os_, ow, yc, wo, g, wr_hi, wr_lo, br)


def _deinterleave_kernel(w_ref, p_ref, g_o, l_o):
    perm = p_ref[...]
    for c in range(w_ref.shape[1] // 256):
        t = _dot(w_ref[:, c * 256:(c + 1) * 256].astype(BF16), perm)
        g_o[:, c * 128:(c + 1) * 128] = t[:, :128].astype(BF16)
        l_o[:, c * 128:(c + 1) * 128] = t[:, 128:].astype(BF16)


def _deinterleave(w, tr, tcw):
    R, C = w.shape
    src = np.arange(256)
    dst = np.where(src % 2 == 0, src // 2, 128 + src // 2)
    perm = np.zeros((256, 256), np.float32)
    perm[src, dst] = 1.0
    out = jax.ShapeDtypeStruct((R, C // 2), BF16)
    return pl.pallas_call(
        _deinterleave_kernel,
        grid=(R // tr, C // tcw),
        in_specs=[
            pl.BlockSpec((tr, tcw), lambda i, j: (i, j)),
            pl.BlockSpec((256, 256), lambda i, j: (0, 0)),
        ],
        out_specs=[pl.BlockSpec((tr, tcw // 2), lambda i, j: (i, j))] * 2,
        out_shape=[out, out],
        compiler_params=_cparams(2),
    )(w, jnp.asarray(perm, BF16))


def _moe_kernel(be_ref, nu_ref, rt_hbm, h2_hbm, w1g_ref, w1l_ref, b1g_ref, b1l_ref,
                w2_ref, b2_ref, y_ref, idx_smem, xf_ref, xb_ref, act_ref, sem_i, sem_g, *, nfa):
    b = pl.program_id(0)
    f = pl.program_id(1)
    blk = xb_ref.shape[0]
    tf = act_ref.shape[2]
    n_used = nu_ref[0]
    used = b < n_used
    slot = b % 2

    def row_copy(tok, r, sl):
        return pltpu.make_async_copy(h2_hbm.at[pl.ds(tok, 1), :], xf_ref.at[sl, pl.ds(r, 1), :],
                                     sem_g.at[sl])

    def start_gather(blk_id, sl):
        cp = pltpu.make_async_copy(rt_hbm.at[blk_id], idx_smem.at[sl], sem_i)
        cp.start()
        cp.wait()

        def issue(r, c):
            row_copy(idx_smem[sl, r], r, sl).start()
            return c

        lax.fori_loop(0, blk, issue, 0, unroll=8)

    @pl.when(used & (f == 0))
    def _():
        @pl.when(b == 0)
        def _():
            start_gather(0, 0)

        def drain(r, c):
            row_copy(0, r, slot).wait()
            return c

        lax.fori_loop(0, blk, drain, 0, unroll=8)
        xb_ref[...] = xf_ref[slot].astype(BF16)

        @pl.when(b + 1 < n_used)
        def _():
            start_gather(b + 1, 1 - slot)

    @pl.when(used & (f < nfa))
    def _():
        x = xb_ref[...]
        glu = jnp.minimum(_dot(x, w1g_ref[...]) + b1g_ref[...], SWIGLU_LIMIT)
        lin = jnp.clip(_dot(x, w1l_ref[...]) + b1l_ref[...], -SWIGLU_LIMIT, SWIGLU_LIMIT)
        act_ref[f] = (glu * _sigmoid(SWIGLU_ALPHA * glu) * (lin + 1.0)).astype(BF16)

    @pl.when(used & (f >= nfa))
    def _():
        y = b2_ref[...] + _dot(act_ref[0], w2_ref[0:tf, :])
        for j in range(1, nfa):
            y += _dot(act_ref[j], w2_ref[j * tf:(j + 1) * tf, :])
        y_ref[...] = y

    @pl.when(jnp.logical_not(used) & (f >= nfa))
    def _():
        y_ref[...] = jnp.zeros(y_ref.shape, F32)


def _moe_experts(block_e, n_used, row_tok2, h2, w1g, w1l, b1g, b1l, w2, b2, tf, tn):
    n_blocks, blk = row_tok2.shape
    E, D, F = w1g.shape
    nfa = F // tf
    nfb = D // tn

    def bb(b, nu):
        return jnp.minimum(b, nu[0] - 1)

    def fa(b, f, nu):
        return jnp.where(b < nu[0], jnp.minimum(f, nfa - 1), nfa - 1)

    def fb(b, f, nu):
        return jnp.where(b < nu[0], jnp.maximum(f - nfa, 0), nfb - 1)

    grid_spec = pltpu.PrefetchScalarGridSpec(
        num_scalar_prefetch=2,
        grid=(n_blocks, nfa + nfb),
        in_specs=[
            pl.BlockSpec(memory_space=pl.ANY),
            pl.BlockSpec(memory_space=pl.ANY),
            pl.BlockSpec((None, D, tf), lambda b, f, be, nu: (be[bb(b, nu)], 0, fa(b, f, nu))),
            pl.BlockSpec((None, D, tf), lambda b, f, be, nu: (be[bb(b, nu)], 0, fa(b, f, nu))),
            pl.BlockSpec((None, 1, tf), lambda b, f, be, nu: (be[bb(b, nu)], 0, fa(b, f, nu))),
            pl.BlockSpec((None, 1, tf), lambda b, f, be, nu: (be[bb(b, nu)], 0, fa(b, f, nu))),
            pl.BlockSpec((None, F, tn), lambda b, f, be, nu: (be[bb(b, nu)], 0, fb(b, f, nu))),
            pl.BlockSpec((None, 1, tn), lambda b, f, be, nu: (be[bb(b, nu)], 0, fb(b, f, nu))),
        ],
        out_specs=pl.BlockSpec((blk, tn), lambda b, f, be, nu: (b, jnp.maximum(f - nfa, 0))),
        scratch_shapes=[
            pltpu.SMEM((2, blk), jnp.int32),
            pltpu.VMEM((2, blk, D), F32),
            pltpu.VMEM((blk, D), BF16),
            pltpu.VMEM((nfa, blk, tf), BF16),
            pltpu.SemaphoreType.DMA,
            pltpu.SemaphoreType.DMA((2,)),
        ],
    )
    return pl.pallas_call(
        functools.partial(_moe_kernel, nfa=nfa),
        grid_spec=grid_spec,
        out_shape=jax.ShapeDtypeStruct((n_blocks * blk, D), F32),
        compiler_params=_cparams(2),
    )(block_e, n_used, row_tok2, h2, w1g, w1l, b1g, b1l, w2, b2)


def _combine_kernel(pos_hbm, ys_hbm, x1_ref, tg_ref, o_ref, idx_smem, buf_ref, sem_i, sem_g):
    i = pl.program_id(0)
    tc = x1_ref.shape[0]
    n = idx_smem.shape[1]
    slot = i % 2

    def row_copy(src, r, sl):
        return pltpu.make_async_copy(ys_hbm.at[pl.ds(src, 1), :], buf_ref.at[sl, pl.ds(r, 1), :],
                                     sem_g.at[sl])

    def start_gather(step, sl):
        cp = pltpu.make_async_copy(pos_hbm.at[step], idx_smem.at[sl], sem_i)
        cp.start()
        cp.wait()

        def issue(r, c):
            row_copy(idx_smem[sl, r], r, sl).start()
            return c

        lax.fori_loop(0, n, issue, 0, unroll=8)

    @pl.when(i == 0)
    def _():
        start_gather(0, 0)

    @pl.when(i + 1 < pl.num_programs(0))
    def _():
        start_gather(i + 1, 1 - slot)

    def drain(r, c):
        row_copy(0, r, slot).wait()
        return c

    lax.fori_loop(0, n, drain, 0, unroll=8)
    acc = x1_ref[...]
    tg = tg_ref[...]
    for k in range(TOP_K):
        acc += buf_ref[slot, k * tc:(k + 1) * tc, :] * tg[:, k:k + 1]
    o_ref[...] = acc


def _combine(pos_tiles, ys, x1, tg, tc):
    T, D = x1.shape
    return pl.pallas_call(
        _combine_kernel,
        grid=(T // tc,),
        in_specs=[
            pl.BlockSpec(memory_space=pl.ANY),
            pl.BlockSpec(memory_space=pl.ANY),
            pl.BlockSpec((tc, D), lambda i: (i, 0)),
            pl.BlockSpec((tc, 128), lambda i: (i, 0)),
        ],
        out_specs=pl.BlockSpec((tc, D), lambda i: (i, 0)),
        out_shape=jax.ShapeDtypeStruct((T, D), F32),
        scratch_shapes=[
            pltpu.SMEM((2, TOP_K * tc), jnp.int32),
            pltpu.VMEM((2, TOP_K * tc, D), F32),
            pltpu.SemaphoreType.DMA,
            pltpu.SemaphoreType.DMA((2,)),
        ],
        compiler_params=_cparams(1),
    )(pos_tiles, ys, x1, tg)


def _route(top_idx, n_experts, blk):
    T = top_idx.shape[0]
    N = T * TOP_K
    e_flat = top_idx.reshape(N)
    tok_flat = jnp.repeat(jnp.arange(T, dtype=jnp.int32), TOP_K)
    onehot = (e_flat[:, None] == jnp.arange(n_experts, dtype=jnp.int32)[None, :]).astype(jnp.int32)
    csum = jnp.cumsum(onehot, axis=0)
    rank = jnp.sum(csum * onehot, axis=1) - 1
    counts = csum[-1]
    padded = ((counts + blk - 1) // blk) * blk
    pend = jnp.cumsum(padded)
    pstart = pend - padded
    dest = pstart[e_flat] + rank
    n_blocks = -(-(N + n_experts * (blk - 1)) // blk)
    rows = n_blocks * blk
    row_tok = jnp.zeros((rows,), jnp.int32).at[dest].set(tok_flat)
    starts = jnp.arange(n_blocks, dtype=jnp.int32) * blk
    block_e = jnp.minimum(jnp.sum((pend[None, :] <= starts[:, None]).astype(jnp.int32), axis=1),
                          n_experts - 1).astype(jnp.int32)
    n_used = (pend[-1:] // blk).astype(jnp.int32)
    return dest, row_tok.reshape(n_blocks, blk), block_e, n_used


def _cmp_to_sel(nc, n_sel):
    ratio = SEL_BLOCK // CMP_STRIDE
    offs = (np.arange(ratio)[:, None] - np.arange(CMP_LEN // CMP_STRIDE)[None, :]).reshape(-1)
    diff = np.arange(nc)[:, None] - ratio * np.arange(n_sel)[None, :]
    return (diff[..., None] == offs).sum(-1).astype(np.float32)


def _pick(n, pref):
    for c in pref:
        if n % c == 0:
            return c
    return n


def kernel(x, mem, positions, mix_norm_g, mem_norm_g, w_in, w_out, gmlp_ws, gmlp_bs, gmlp_vg,
           nsa_cmp_pos, nsa_ck1, nsa_ck2, nsa_cv1, nsa_cv2, nsa_q_g, nsa_k_g,
           xattn_wkv, xattn_q_g, xattn_k_g, ffn_norm_g, w_router, b_router,
           w_e1, b_e1, w_e2, b_e2):
    B, S, D = x.shape
    T = B * S
    depth = w_in.shape[0]
    n_experts = w_router.shape[-1]
    NC = S // CMP_STRIDE
    n_sel = S // SEL_BLOCK
    tm = _pick(S, (512, 256, 128))
    tk = _pick(S, (512, 256, 128))

    half = HEAD_DIM // 2
    inv = ROPE_THETA ** (-jnp.arange(half, dtype=F32) * 2.0 / HEAD_DIM)
    ang = (positions.astype(F32).reshape(T, 1) * jnp.concatenate([inv, inv])[None, :])
    c2st = jnp.asarray(_cmp_to_sel(NC, n_sel).T, BF16)

    xs = x.reshape(T, D)
    for l in range(depth):
        wi = w_in[l]
        n_gate = 3 * NSA_REP
        zpad = jnp.zeros((D, 128 - n_gate), wi.dtype)
        w_r = jnp.concatenate(
            [wi[:, :3584], wi[:, 3608:4120],
             wi[:, 3584:3584 + n_gate], zpad, wi[:, 3584 + n_gate:3608], zpad], axis=1).astype(BF16)

        proj = _in_proj(xs, mix_norm_g[l][None, :], w_r, tm, PROJ_COLS // 2)

        bias_full = jnp.repeat(jnp.transpose(gmlp_bs[l]), HEAD_DIM, axis=1)
        y_a = _gmlp(proj, gmlp_ws[l], bias_full, gmlp_vg[l].reshape(1, GMLP_WIDTH), T, tm)

        qn, qr, ks, vs, kw, vw = _prep(proj, ang, nsa_q_g[l][None, :], nsa_k_g[l][None, :], B, S, tm)

        w1 = jnp.stack([nsa_ck1[l], nsa_cv1[l]]).astype(BF16)
        w2 = jnp.stack([nsa_ck2[l], nsa_cv2[l]]).astype(BF16)
        cmp_kv = _compress(proj.reshape(B, NC, CMP_STRIDE, PROJ_COLS), w1, w2,
                           nsa_cmp_pos[l], nsa_k_g[l][None, :], B, NC)

        o_c, sel = _cmp_attn(qn, cmp_kv, proj, c2st, B, S, Q_BLOCK)
        o_s = _sel_attn(qr, ks, vs, proj, sel, B, S, _pick(S, (256, 128)), _pick(S, (1024, 512, 256)),
                        _pick(S, (512, 256)))
        o_w = _win_attn(qr, kw, vw, proj, B, S, _pick(S, (256, 128)))

        km, vm = _mem_kv(mem, mem_norm_g[l][None, :], xattn_wkv[l].astype(BF16), xattn_k_g[l][None, :])
        y_c = _xattn(proj, km, vm, xattn_q_g[l][None, :], B, S, tm)

        wr = jnp.pad(w_router[l], ((0, 0), (0, 128 - n_experts)))
        wr_hi = wr.astype(BF16)
        wr_lo = (wr - wr_hi.astype(F32)).astype(BF16)
        br = jnp.pad(b_router[l], (0, 128 - n_experts))[None, :]
        x1, h2, ti, tg = _out_proj(xs, y_a, o_c, o_s, o_w, y_c, w_out[l].astype(BF16),
                                   ffn_norm_g[l][None, :], wr_hi, wr_lo, br, n_experts,
                                   _pick(S, (256, 128)))

        top_idx = ti[:, :TOP_K].astype(jnp.int32)
        dest, row_tok2, block_e, n_used = _route(top_idx, n_experts, MOE_BLOCK)

        d_ff = w_e2.shape[2]
        w1g, w1l = _deinterleave(w_e1[l].reshape(n_experts * D, 2 * d_ff), 512, _pick(2 * d_ff, (2048, 1024, 512, 256)))
        w1g = w1g.reshape(n_experts, D, d_ff)
        w1l = w1l.reshape(n_experts, D, d_ff)
        b1g = b_e1[l][:, None, 0::2]
        b1l = b_e1[l][:, None, 1::2]
        ys = _moe_experts(block_e, n_used, row_tok2, h2, w1g, w1l, b1g, b1l,
                          w_e2[l].astype(BF16), b_e2[l][:, None, :], _pick(d_ff, (1024, 512)),
                          _pick(D, (1024, 512)))

        tc = 128
        pos_tiles = dest.reshape(T // tc, tc, TOP_K).transpose(0, 2, 1).reshape(T // tc, TOP_K * tc)
        xs = _combine(pos_tiles, ys, x1, tg, tc)
    return xs.reshape(B, S, D)
```

```python
import functools

import numpy as np
import jax
import jax.numpy as jnp
from jax import lax
from jax.experimental import pallas as pl
from jax.experimental.pallas import tpu as pltpu

F32 = jnp.float32
BF16 = jnp.bfloat16

HEAD_DIM = 128
GMLP_GROUPS = 4
NSA_HEADS = 8
NSA_KV_HEADS = 2
NSA_REP = NSA_HEADS // NSA_KV_HEADS
XATTN_HEADS = 4
GMLP_WIDTH = GMLP_GROUPS * HEAD_DIM
NSA_WIDTH = NSA_HEADS * HEAD_DIM
XATTN_WIDTH = XATTN_HEADS * HEAD_DIM
GMLP_CHUNK = 128
CMP_LEN = 32
CMP_STRIDE = 16
SEL_BLOCK = 64
SEL_TOPK = 16
WINDOW = 512
Q_BLOCK = 128
TOP_K = 4
SWIGLU_ALPHA = 1.702
SWIGLU_LIMIT = 7.0
MOE_BLOCK = 512
ROPE_THETA = 10000.0
EPS = 1e-6
NEG_BIG = -1e30
MASK_BIG = 2.0 ** 100

COL_UV = 0
COL_Q = 8
COL_KV = 16
COL_QX = 28
COL_GATE = 32
N_COL_BLOCKS = 34
PROJ_COLS = N_COL_BLOCKS * 128

VMEM_LIMIT = 56 * 1024 * 1024


def _cparams(n_axes):
    return pltpu.CompilerParams(
        dimension_semantics=("arbitrary",) * n_axes, vmem_limit_bytes=VMEM_LIMIT)


def _rms(x, g):
    ms = jnp.mean(x * x, axis=-1, keepdims=True)
    return x * lax.rsqrt(ms + EPS) * g


def _gelu_tanh(x):
    c = np.float32(np.sqrt(2.0 / np.pi))
    return 0.5 * x * (1.0 + jnp.tanh(c * (x + 0.044715 * (x * x * x))))


def _sigmoid(x):
    return 1.0 / (1.0 + jnp.exp(-x))


def _dot(a, b):
    return jnp.dot(a, b, preferred_element_type=F32)


def _dot_nt(a, b):
    return lax.dot_general(a, b, (((1,), (1,)), ((), ())), preferred_element_type=F32)


def _inproj_kernel(x_ref, g_ref, w_ref, o_ref, h_ref):
    @pl.when(pl.program_id(1) == 0)
    def _():
        h_ref[...] = _rms(x_ref[...], g_ref[...]).astype(BF16)

    o_ref[...] = _dot(h_ref[...], w_ref[...])


def _in_proj(x2, g, w_r, tm, tn):
    T, D = x2.shape
    NC = w_r.shape[1]
    return pl.pallas_call(
        _inproj_kernel,
        grid=(T // tm, NC // tn),
        in_specs=[
            pl.BlockSpec((tm, D), lambda i, j: (i, 0)),
            pl.BlockSpec((1, D), lambda i, j: (0, 0)),
            pl.BlockSpec((D, tn), lambda i, j: (0, j)),
        ],
        out_specs=pl.BlockSpec((tm, tn), lambda i, j: (i, j)),
        out_shape=jax.ShapeDtypeStruct((T, NC), F32),
        scratch_shapes=[pltpu.VMEM((tm, D), BF16)],
        compiler_params=_cparams(2),
    )(x2, g, w_r)


def _prep_kernel(q_ref, sel_ref, win_ref, ang_ref, qg_ref, kg_ref,
                 qn_o, qr_o, ks_o, vs_o, kw_o, vw_o):
    ang = ang_ref[...]
    lane = lax.broadcasted_iota(jnp.int32, (1, HEAD_DIM), 1)
    cosf = jnp.cos(ang)
    sinf = jnp.sin(ang) * jnp.where(lane < HEAD_DIM // 2, -1.0, 1.0)
    scale = np.float32(HEAD_DIM ** -0.5 * np.log2(np.e))

    def rope(x):
        return x * cosf + pltpu.roll(x, HEAD_DIM // 2, axis=1) * sinf

    qg = qg_ref[...]
    kg = kg_ref[...]
    for h in range(NSA_HEADS):
        sl = slice(h * HEAD_DIM, (h + 1) * HEAD_DIM)
        n = _rms(q_ref[:, sl], qg)
        qn_o[:, sl] = (n * scale).astype(BF16)
        qr_o[:, sl] = (rope(n) * scale).astype(BF16)
    for g in range(NSA_KV_HEADS):
        sl = slice(g * HEAD_DIM, (g + 1) * HEAD_DIM)
        sv = slice((NSA_KV_HEADS + g) * HEAD_DIM, (NSA_KV_HEADS + g + 1) * HEAD_DIM)
        ks_o[g] = rope(_rms(sel_ref[:, sl], kg)).astype(BF16)
        vs_o[g] = sel_ref[:, sv].astype(BF16)
        kw_o[g] = rope(_rms(win_ref[:, sl], kg)).astype(BF16)
        vw_o[g] = win_ref[:, sv].astype(BF16)


def _prep(proj, ang, q_g, k_g, B, S, tm):
    T = B * S
    nsb = S // tm
    kv_shape = jax.ShapeDtypeStruct((B, NSA_KV_HEADS, S, HEAD_DIM), BF16)
    kv_spec = pl.BlockSpec((None, NSA_KV_HEADS, tm, HEAD_DIM),
                           lambda i: (i // nsb, 0, i % nsb, 0))
    return pl.pallas_call(
        _prep_kernel,
        grid=(T // tm,),
        in_specs=[
            pl.BlockSpec((tm, NSA_WIDTH), lambda i: (i, COL_Q * 128 // NSA_WIDTH)),
            pl.BlockSpec((tm, 512), lambda i: (i, (COL_KV + 4) * 128 // 512)),
            pl.BlockSpec((tm, 512), lambda i: (i, (COL_KV + 8) * 128 // 512)),
            pl.BlockSpec((tm, HEAD_DIM), lambda i: (i, 0)),
            pl.BlockSpec((1, HEAD_DIM), lambda i: (0, 0)),
            pl.BlockSpec((1, HEAD_DIM), lambda i: (0, 0)),
        ],
        out_specs=[
            pl.BlockSpec((tm, NSA_WIDTH), lambda i: (i, 0)),
            pl.BlockSpec((tm, NSA_WIDTH), lambda i: (i, 0)),
            kv_spec, kv_spec, kv_spec, kv_spec,
        ],
        out_shape=[
            jax.ShapeDtypeStruct((T, NSA_WIDTH), BF16),
            jax.ShapeDtypeStruct((T, NSA_WIDTH), BF16),
            kv_shape, kv_shape, kv_shape, kv_shape,
        ],
        compiler_params=_cparams(1),
    )(proj, proj, proj, ang, q_g, k_g)


def _gmlp_kernel(uv_ref, ws_ref, bias_ref, gv_ref, o_ref, *, n_chunks):
    row = lax.broadcasted_iota(jnp.int32, (GMLP_CHUNK, GMLP_CHUNK), 0)
    col = lax.broadcasted_iota(jnp.int32, (GMLP_CHUNK, GMLP_CHUNK), 1)
    causal = col <= row
    for g in range(GMLP_GROUPS):
        sl = slice(g * HEAD_DIM, (g + 1) * HEAD_DIM)
        sv = slice(GMLP_WIDTH + g * HEAD_DIM, GMLP_WIDTH + (g + 1) * HEAD_DIM)
        w = jnp.where(causal, ws_ref[g], 0.0).astype(BF16)
        u = _gelu_tanh(uv_ref[:, sl])
        v = _rms(_gelu_tanh(uv_ref[:, sv]), gv_ref[:, sl]).astype(BF16)
        bias = bias_ref[:, sl]
        for c in range(n_chunks):
            rs = slice(c * GMLP_CHUNK, (c + 1) * GMLP_CHUNK)
            vs = _dot(w, v[rs]) + bias
            o_ref[rs, sl] = (u[rs] * vs).astype(BF16)


def _gmlp(proj, ws, bias_full, gv, T, tm):
    return pl.pallas_call(
        functools.partial(_gmlp_kernel, n_chunks=tm // GMLP_CHUNK),
        grid=(T // tm,),
        in_specs=[
            pl.BlockSpec((tm, 2 * GMLP_WIDTH), lambda i: (i, 0)),
            pl.BlockSpec((GMLP_GROUPS, GMLP_CHUNK, GMLP_CHUNK), lambda i: (0, 0, 0)),
            pl.BlockSpec((GMLP_CHUNK, GMLP_WIDTH), lambda i: (0, 0)),
            pl.BlockSpec((1, GMLP_WIDTH), lambda i: (0, 0)),
        ],
        out_specs=pl.BlockSpec((tm, GMLP_WIDTH), lambda i: (i, 0)),
        out_shape=jax.ShapeDtypeStruct((T, GMLP_WIDTH), BF16),
        compiler_params=_cparams(1),
    )(proj, ws, bias_full, gv)


def _compress_kernel(x_ref, w1_ref, w2_ref, pos_ref, kg_ref, o_ref):
    nc = x_ref.shape[0]
    kind = pl.program_id(1)
    acc_a = jnp.zeros((nc, HEAD_DIM), F32)
    acc_b = jnp.zeros((nc, HEAD_DIM), F32)
    for j in range(CMP_STRIDE):
        xj = x_ref[:, j, :]
        wa = w1_ref[j * HEAD_DIM:(j + 1) * HEAD_DIM, :]
        wb = w1_ref[(CMP_STRIDE + j) * HEAD_DIM:(CMP_STRIDE + j + 1) * HEAD_DIM, :]
        acc_a += _dot((xj + pos_ref[j:j + 1, :]).astype(BF16), wa)
        acc_b += _dot((xj + pos_ref[CMP_STRIDE + j:CMP_STRIDE + j + 1, :]).astype(BF16), wb)
    hidden = acc_a + pltpu.roll(acc_b, nc - 1, axis=0)
    out = _dot(_gelu_tanh(hidden).astype(BF16), w2_ref[...])
    o_ref[...] = jnp.where(kind == 0, _rms(out, kg_ref[...]), out).astype(BF16)


def _compress(proj4, w1, w2, cmp_pos, k_g, B, NC):
    return pl.pallas_call(
        _compress_kernel,
        grid=(B, 2, NSA_KV_HEADS),
        in_specs=[
            pl.BlockSpec((None, NC, CMP_STRIDE, HEAD_DIM),
                         lambda b, k, g: (b, 0, 0, COL_KV + 2 * k + g)),
            pl.BlockSpec((None, CMP_LEN * HEAD_DIM, HEAD_DIM), lambda b, k, g: (k, 0, 0)),
            pl.BlockSpec((None, HEAD_DIM, HEAD_DIM), lambda b, k, g: (k, 0, 0)),
            pl.BlockSpec((CMP_LEN, HEAD_DIM), lambda b, k, g: (0, 0)),
            pl.BlockSpec((1, HEAD_DIM), lambda b, k, g: (0, 0)),
        ],
        out_specs=pl.BlockSpec((None, None, None, NC, HEAD_DIM), lambda b, k, g: (b, k, g, 0, 0)),
        out_shape=jax.ShapeDtypeStruct((B, 2, NSA_KV_HEADS, NC, HEAD_DIM), BF16),
        compiler_params=_cparams(3),
    )(proj4, w1, w2, cmp_pos, k_g)


def _cmp_attn_kernel(q_ref, kc_ref, vc_ref, gl_ref, c2st_ref, oc_ref, sel_ref, q_sc, imp_sc,
                     *, k_top, widths):
    i = pl.program_id(2)
    tq = q_ref.shape[0]
    n_sel = c2st_ref.shape[0]
    t = i * tq + lax.broadcasted_iota(jnp.int32, (tq, 1), 0)
    gl = gl_ref[...]
    for r in range(NSA_REP):
        q_sc[r * tq:(r + 1) * tq, :] = q_ref[:, r * HEAD_DIM:(r + 1) * HEAD_DIM]

    def attend(nc):
        cmp_end = lax.broadcasted_iota(jnp.int32, (1, nc), 1) * CMP_STRIDE + (CMP_LEN - 1)
        bias = jnp.where(cmp_end <= t, 0.0, -jnp.inf)
        s = _dot_nt(q_sc[...], kc_ref[0:nc, :]) + jnp.concatenate([bias] * NSA_REP, axis=0)
        m = jnp.max(s, axis=-1, keepdims=True)
        m = jnp.where(m > -jnp.inf, m, 0.0)
        e = jnp.exp2(s - m)
        d = jnp.sum(e, axis=-1, keepdims=True)
        p = e * (1.0 / jnp.where(d > 0, d, 1.0))
        o = _dot(p.astype(BF16), vc_ref[0:nc, :])
        pc_sum = jnp.zeros((tq, nc), F32)
        for r in range(NSA_REP):
            rs = slice(r * tq, (r + 1) * tq)
            gate = _sigmoid(gl[:, 3 * r:3 * r + 1])
            oc_ref[:, r * HEAD_DIM:(r + 1) * HEAD_DIM] = o[rs, :] * gate
            pc_sum += p[rs, :]
        hi = pc_sum.astype(BF16)
        lo = (pc_sum - hi.astype(F32)).astype(BF16)
        imp_sc[...] = _dot_nt(c2st_ref[:, 0:nc], hi) + _dot_nt(c2st_ref[:, 0:nc], lo)

    needed = (i * tq + tq - CMP_LEN) // CMP_STRIDE + 1
    for j, w in enumerate(widths):
        conds = ([needed <= w] if j + 1 < len(widths) else []) + ([needed > widths[j - 1]] if j else [])
        if conds:
            pl.when(functools.reduce(jnp.logical_and, conds))(functools.partial(attend, w))
        else:
            attend(w)

    imp = imp_sc[...]
    sid = lax.broadcasted_iota(jnp.int32, (n_sel, 1), 0)
    sid_f = sid.astype(F32)
    tt = i * tq + lax.broadcasted_iota(jnp.int32, (1, tq), 1)
    cur = tt // SEL_BLOCK
    valid = sid * SEL_BLOCK <= tt
    forced = (sid == 0) | (sid == cur) | (sid == cur - 1)
    score = jnp.where(valid, jnp.where(forced, jnp.inf, imp), -jnp.inf)
    sel = jnp.zeros((n_sel, tq), F32)
    for _ in range(k_top):
        m = jnp.max(score, axis=0, keepdims=True)
        idx = jnp.min(jnp.where(score == m, sid_f, np.float32(n_sel)), axis=0, keepdims=True)
        pick = sid_f == idx
        sel = jnp.where(pick, 1.0, sel)
        score = jnp.where(pick, -jnp.inf, score)
    selm = jnp.transpose(sel - 1.0)
    pad = sel_ref.shape[1] - n_sel
    if pad:
        selm = jnp.concatenate([selm, jnp.full((tq, pad), -1.0, F32)], axis=1)
    sel_ref[...] = selm


def _cmp_attn(qn, cmp_kv, proj, c2st, B, S, tq):
    T = B * S
    nqb = S // tq
    NC = cmp_kv.shape[3]
    n_sel = S // SEL_BLOCK
    n_sel_pad = -(-n_sel // 128) * 128
    gw = NSA_REP * HEAD_DIM
    return pl.pallas_call(
        functools.partial(_cmp_attn_kernel, k_top=min(SEL_TOPK, n_sel),
                          widths=tuple(NC * (j + 1) // 4 for j in range(4)) if NC % 1024 == 0 else (NC,)),
        grid=(B, NSA_KV_HEADS, nqb),
        in_specs=[
            pl.BlockSpec((tq, gw), lambda b, g, i: (b * nqb + i, g)),
            pl.BlockSpec((None, None, None, NC, HEAD_DIM), lambda b, g, i: (b, 0, g, 0, 0)),
            pl.BlockSpec((None, None, None, NC, HEAD_DIM), lambda b, g, i: (b, 1, g, 0, 0)),
            pl.BlockSpec((tq, 128), lambda b, g, i: (b * nqb + i, COL_GATE + g)),
            pl.BlockSpec((n_sel, NC), lambda b, g, i: (0, 0)),
        ],
        out_specs=[
            pl.BlockSpec((tq, gw), lambda b, g, i: (b * nqb + i, g)),
            pl.BlockSpec((None, None, tq, n_sel_pad), lambda b, g, i: (b, g, i, 0)),
        ],
        out_shape=[
            jax.ShapeDtypeStruct((T, NSA_WIDTH), F32),
            jax.ShapeDtypeStruct((B, NSA_KV_HEADS, S, n_sel_pad), F32),
        ],
        scratch_shapes=[pltpu.VMEM((NSA_REP * tq, HEAD_DIM), BF16), pltpu.VMEM((n_sel, tq), F32)],
        compiler_params=_cparams(3),
    )(qn, cmp_kv, cmp_kv, proj, c2st)


def _sel_attn_kernel(q_ref, k_ref, v_ref, gl_ref, sel_ref, e0_ref, o_ref,
                     q_sc, m_sc, l_sc, acc_sc, *, tk, tkd):
    i = pl.program_id(2)
    tq = q_ref.shape[0]
    s0 = i * tq
    t = s0 + lax.broadcasted_iota(jnp.int32, (tq, 1), 0)
    for r in range(NSA_REP):
        q_sc[r * tq:(r + 1) * tq, :] = q_ref[:, r * HEAD_DIM:(r + 1) * HEAD_DIM]
    m_sc[...] = jnp.full(m_sc.shape, NEG_BIG, F32)
    l_sc[...] = jnp.zeros(l_sc.shape, F32)
    acc_sc[...] = jnp.zeros(acc_sc.shape, F32)
    def step(koff, width, diagonal):
        koff = pl.multiple_of(koff, width)
        k = k_ref[pl.ds(koff, width), :]
        v = v_ref[pl.ds(koff, width), :]
        c0 = koff // SEL_BLOCK
        slab = sel_ref[:, pl.ds(pl.multiple_of((c0 // 128) * 128, 128), 128)]
        rolled = pltpu.roll(slab, (128 - c0 % 128) % 128, axis=1)
        bias = _dot(rolled.astype(BF16), e0_ref[:, :width])
        if diagonal:
            key = koff + lax.broadcasted_iota(jnp.int32, (1, width), 1)
            bias = jnp.where(key <= t, bias, -MASK_BIG)
        s = _dot_nt(q_sc[...], k) + jnp.concatenate([bias] * NSA_REP, axis=0)
        m_old = m_sc[...]
        m_new = jnp.maximum(m_old, jnp.max(s, axis=-1, keepdims=True))
        alpha = jnp.exp2(m_old - m_new)
        p = jnp.exp2(s - jnp.concatenate([m_new] * (width // 128), axis=1))
        l_sc[...] = alpha * l_sc[...] + jnp.sum(p, axis=-1, keepdims=True)
        acc_sc[...] = alpha * acc_sc[...] + _dot(p.astype(BF16), v)
        m_sc[...] = m_new

    def body(j, carry):
        step(j * tk, tk, False)
        return carry

    n_wide = s0 // tk
    lax.fori_loop(0, n_wide, body, 0)
    tail = n_wide * tk
    n_narrow = (s0 - tail) // tkd
    for u in range(tk // tkd - 1):
        @pl.when(u < n_narrow)
        def _():
            step(tail + u * tkd, tkd, False)

    step(tail + n_narrow * tkd, tkd, True)
    gl = gl_ref[...]
    for r in range(NSA_REP):
        rs = slice(r * tq, (r + 1) * tq)
        gate = _sigmoid(gl[:, 3 * r + 1:3 * r + 2])
        o_ref[:, r * HEAD_DIM:(r + 1) * HEAD_DIM] = acc_sc[rs, :] / l_sc[rs, :] * gate


def _sel_attn(qr, ks, vs, proj, sel, B, S, tq, tk, tkd):
    assert tk % tkd == 0 and tkd % tq == 0 and tkd % SEL_BLOCK == 0 and tk // SEL_BLOCK <= 128
    T = B * S
    nqb = S // tq
    n_sel = S // SEL_BLOCK
    gw = NSA_REP * HEAD_DIM
    n_sel_pad = sel.shape[-1]
    e0 = np.where(np.arange(128)[:, None] == np.arange(tk)[None, :] // SEL_BLOCK, MASK_BIG, 0.0)
    return pl.pallas_call(
        functools.partial(_sel_attn_kernel, tk=tk, tkd=tkd),
        grid=(B, NSA_KV_HEADS, nqb),
        in_specs=[
            pl.BlockSpec((tq, gw), lambda b, g, i: (b * nqb + i, g)),
            pl.BlockSpec((None, None, S, HEAD_DIM), lambda b, g, i: (b, g, 0, 0)),
            pl.BlockSpec((None, None, S, HEAD_DIM), lambda b, g, i: (b, g, 0, 0)),
            pl.BlockSpec((tq, 128), lambda b, g, i: (b * nqb + i, COL_GATE + g)),
            pl.BlockSpec((None, None, tq, n_sel_pad), lambda b, g, i: (b, g, i, 0)),
            pl.BlockSpec((128, tk), lambda b, g, i: (0, 0)),
        ],
        out_specs=pl.BlockSpec((tq, gw), lambda b, g, i: (b * nqb + i, g)),
        out_shape=jax.ShapeDtypeStruct((T, NSA_WIDTH), F32),
        scratch_shapes=[
            pltpu.VMEM((NSA_REP * tq, HEAD_DIM), BF16),
            pltpu.VMEM((NSA_REP * tq, 128), F32),
            pltpu.VMEM((NSA_REP * tq, 128), F32),
            pltpu.VMEM((NSA_REP * tq, HEAD_DIM), F32),
        ],
        compiler_params=_cparams(3),
    )(qr, ks, vs, proj, sel, jnp.asarray(e0, BF16))


def _win_attn_kernel(q_ref, k_ref, v_ref, gl_ref, o_ref, q_sc, *, wk):
    i = pl.program_id(2)
    tq = q_ref.shape[0]
    s0 = i * tq
    t = s0 + lax.broadcasted_iota(jnp.int32, (tq, 1), 0)
    for r in range(NSA_REP):
        q_sc[r * tq:(r + 1) * tq, :] = q_ref[:, r * HEAD_DIM:(r + 1) * HEAD_DIM]
    ws = pl.multiple_of(jnp.maximum(s0 - WINDOW, 0), tq)
    k = k_ref[pl.ds(ws, wk), :]
    v = v_ref[pl.ds(ws, wk), :]
    kp = ws + lax.broadcasted_iota(jnp.int32, (1, wk), 1)
    bias = jnp.where(kp <= t, jnp.where(kp > t - WINDOW, 0.0, NEG_BIG), NEG_BIG)
    s = _dot_nt(q_sc[...], k) + jnp.concatenate([bias] * NSA_REP, axis=0)
    m = jnp.max(s, axis=-1, keepdims=True)
    e = jnp.exp2(s - m)
    p = e * (1.0 / jnp.sum(e, axis=-1, keepdims=True))
    o = _dot(p.astype(BF16), v)
    gl = gl_ref[...]
    for r in range(NSA_REP):
        gate = _sigmoid(gl[:, 3 * r + 2:3 * r + 3])
        o_ref[:, r * HEAD_DIM:(r + 1) * HEAD_DIM] = o[r * tq:(r + 1) * tq, :] * gate


def _win_attn(qr, kw, vw, proj, B, S, tq):
    T = B * S
    nqb = S // tq
    gw = NSA_REP * HEAD_DIM
    wk = min(WINDOW + tq, S)
    return pl.pallas_call(
        functools.partial(_win_attn_kernel, wk=wk),
        grid=(B, NSA_KV_HEADS, nqb),
        in_specs=[
            pl.BlockSpec((tq, gw), lambda b, g, i: (b * nqb + i, g)),
            pl.BlockSpec((None, None, S, HEAD_DIM), lambda b, g, i: (b, g, 0, 0)),
            pl.BlockSpec((None, None, S, HEAD_DIM), lambda b, g, i: (b, g, 0, 0)),
            pl.BlockSpec((tq, 128), lambda b, g, i: (b * nqb + i, COL_GATE + g)),
        ],
        out_specs=pl.BlockSpec((tq, gw), lambda b, g, i: (b * nqb + i, g)),
        out_shape=jax.ShapeDtypeStruct((T, NSA_WIDTH), F32),
        scratch_shapes=[pltpu.VMEM((NSA_REP * tq, HEAD_DIM), BF16)],
        compiler_params=_cparams(3),
    )(qr, kw, vw, proj)


def _memkv_kernel(mem_ref, g_ref, w_ref, kg_ref, k_o, v_o):
    h = _rms(mem_ref[...], g_ref[...]).astype(BF16)
    kv = _dot(h, w_ref[...])
    kg = kg_ref[...]
    for hd in range(XATTN_HEADS):
        sl = slice(hd * HEAD_DIM, (hd + 1) * HEAD_DIM)
        sv = slice(XATTN_WIDTH + hd * HEAD_DIM, XATTN_WIDTH + (hd + 1) * HEAD_DIM)
        k_o[:, sl] = _rms(kv[:, sl], kg).astype(BF16)
        v_o[:, sl] = kv[:, sv].astype(BF16)


def _mem_kv(mem, g, wkv, k_g):
    B, M, D = mem.shape
    shp = jax.ShapeDtypeStruct((B, M, XATTN_WIDTH), BF16)
    spec = pl.BlockSpec((None, M, XATTN_WIDTH), lambda b: (b, 0, 0))
    return pl.pallas_call(
        _memkv_kernel,
        grid=(B,),
        in_specs=[
            pl.BlockSpec((None, M, D), lambda b: (b, 0, 0)),
            pl.BlockSpec((1, D), lambda b: (0, 0)),
            pl.BlockSpec((D, 2 * XATTN_WIDTH), lambda b: (0, 0)),
            pl.BlockSpec((1, HEAD_DIM), lambda b: (0, 0)),
        ],
        out_specs=[spec, spec],
        out_shape=[shp, shp],
        compiler_params=_cparams(1),
    )(mem, g, wkv, k_g)


def _xattn_kernel(q_ref, k_ref, v_ref, qg_ref, o_ref):
    scale = np.float32(HEAD_DIM ** -0.5)
    qg = qg_ref[...]
    for hd in range(XATTN_HEADS):
        sl = slice(hd * HEAD_DIM, (hd + 1) * HEAD_DIM)
        q = (_rms(q_ref[:, sl], qg) * scale).astype(BF16)
        s = _dot_nt(q, k_ref[:, sl])
        m = jnp.max(s, axis=-1, keepdims=True)
        e = jnp.exp(s - m)
        p = e * (1.0 / jnp.sum(e, axis=-1, keepdims=True))
        o_ref[:, sl] = _dot(p.astype(BF16), v_ref[:, sl]).astype(BF16)


def _xattn(proj, km, vm, q_g, B, S, tm):
    T = B * S
    nsb = S // tm
    M = km.shape[1]
    return pl.pallas_call(
        _xattn_kernel,
        grid=(T // tm,),
        in_specs=[
            pl.BlockSpec((tm, XATTN_WIDTH), lambda i: (i, COL_QX * 128 // XATTN_WIDTH)),
            pl.BlockSpec((None, M, XATTN_WIDTH), lambda i: (i // nsb, 0, 0)),
            pl.BlockSpec((None, M, XATTN_WIDTH), lambda i: (i // nsb, 0, 0)),
            pl.BlockSpec((1, HEAD_DIM), lambda i: (0, 0)),
        ],
        out_specs=pl.BlockSpec((tm, XATTN_WIDTH), lambda i: (i, 0)),
        out_shape=jax.ShapeDtypeStruct((T, XATTN_WIDTH), BF16),
        compiler_params=_cparams(1),
    )(proj, km, vm, q_g)


def _outproj_kernel(x_ref, ya_ref, oc_ref, os_ref, ow_ref, yc_ref, wo_ref, g_ref,
                    wrh_ref, wrl_ref, br_ref, x1_o, h2_o, ti_o, tg_o, *, n_experts):
    yb = (oc_ref[...] + os_ref[...] + ow_ref[...]).astype(BF16)
    x1 = x_ref[...]
    x1 += _dot(ya_ref[...], wo_ref[0:GMLP_WIDTH, :])
    x1 += _dot(yb, wo_ref[GMLP_WIDTH:GMLP_WIDTH + NSA_WIDTH, :])
    x1 += _dot(yc_ref[...], wo_ref[GMLP_WIDTH + NSA_WIDTH:, :])
    x1_o[...] = x1
    h2 = _rms(x1, g_ref[...])
    h2_o[...] = h2
    hi = h2.astype(BF16)
    lo = (h2 - hi.astype(F32)).astype(BF16)
    logits = (_dot(hi, wrh_ref[...]) + _dot(lo, wrh_ref[...]) + _dot(hi, wrl_ref[...])
              + br_ref[...])
    lane = lax.broadcasted_iota(jnp.int32, (1, 128), 1)
    lane_f = lane.astype(F32)
    lg = jnp.where(lane < n_experts, logits, -jnp.inf)
    ti = jnp.zeros(logits.shape, F32)
    tg = jnp.zeros(logits.shape, F32)
    denom = jnp.zeros((logits.shape[0], 1), F32)
    v0 = None
    for k in range(TOP_K):
        m = jnp.max(lg, axis=-1, keepdims=True)
        ix = jnp.min(jnp.where(lg == m, lane_f, 128.0), axis=-1, keepdims=True)
        if v0 is None:
            v0 = m
        e = jnp.exp(m - v0)
        denom += e
        ti = jnp.where(lane == k, ix, ti)
        tg = jnp.where(lane == k, e, tg)
        lg = jnp.where(lane_f == ix, -jnp.inf, lg)
    ti_o[...] = ti
    tg_o[...] = tg / denom


def _out_proj(x2, ya, oc, os_, ow, yc, wo, g, wr_hi, wr_lo, br, n_experts, tm):
    T, D = x2.shape
    row = lambda w: pl.BlockSpec((tm, w), lambda i: (i, 0))
    full = lambda a: pl.BlockSpec(a.shape, lambda i: (0,) * a.ndim)
    return pl.pallas_call(
        functools.partial(_outproj_kernel, n_experts=n_experts),
        grid=(T // tm,),
        in_specs=[row(D), row(GMLP_WIDTH), row(NSA_WIDTH), row(NSA_WIDTH), row(NSA_WIDTH),
                  row(XATTN_WIDTH), full(wo), full(g), full(wr_hi), full(wr_lo), full(br)],
        out_specs=[row(D), row(D), row(128), row(128)],
        out_shape=[
            jax.ShapeDtypeStruct((T, D), F32),
            jax.ShapeDtypeStruct((T, D), F32),
            jax.ShapeDtypeStruct((T, 128), F32),
            jax.ShapeDtypeStruct((T, 128), F32),
        ],
        compiler_params=_cparams(1),
    )(x2, ya, oc, os_, ow, yc, wo, g, wr_hi, wr_lo, br)


def _deinterleave_kernel(w_ref, p_ref, g_o, l_o):
    perm = p_ref[...]
    for c in range(w_ref.shape[1] // 256):
        t = _dot(w_ref[:, c * 256:(c + 1) * 256].astype(BF16), perm)
        g_o[:, c * 128:(c + 1) * 128] = t[:, :128].astype(BF16)
        l_o[:, c * 128:(c + 1) * 128] = t[:, 128:].astype(BF16)


def _deinterleave(w, tr, tcw):
    R, C = w.shape
    src = np.arange(256)
    dst = np.where(src % 2 == 0, src // 2, 128 + src // 2)
    perm = np.zeros((256, 256), np.float32)
    perm[src, dst] = 1.0
    out = jax.ShapeDtypeStruct((R, C // 2), BF16)
    return pl.pallas_call(
        _deinterleave_kernel,
        grid=(R // tr, C // tcw),
        in_specs=[
            pl.BlockSpec((tr, tcw), lambda i, j: (i, j)),
            pl.BlockSpec((256, 256), lambda i, j: (0, 0)),
        ],
        out_specs=[pl.BlockSpec((tr, tcw // 2), lambda i, j: (i, j))] * 2,
        out_shape=[out, out],
        compiler_params=_cparams(2),
    )(w, jnp.asarray(perm, BF16))


def _moe_kernel(be_ref, nu_ref, rt_hbm, h2_hbm, w1g_ref, w1l_ref, b1g_ref, b1l_ref,
                w2_ref, b2_ref, y_ref, idx_smem, xf_ref, xb_ref, sem_i, sem_g, *, tn):
    b = pl.program_id(0)
    f = pl.program_id(1)
    blk = xb_ref.shape[0]
    n_used = nu_ref[0]
    used = b < n_used
    slot = b % 2

    def row_copy(tok, r, sl):
        return pltpu.make_async_copy(h2_hbm.at[pl.ds(tok, 1), :], xf_ref.at[sl, pl.ds(r, 1), :],
                                     sem_g.at[sl])

    def start_gather(blk_id, sl):
        cp = pltpu.make_async_copy(rt_hbm.at[blk_id], idx_smem.at[sl], sem_i)
        cp.start()
        cp.wait()

        def issue(r, c):
            row_copy(idx_smem[sl, r], r, sl).start()
            return c

        lax.fori_loop(0, blk, issue, 0, unroll=8)

    @pl.when(used & (f == 0))
    def _():
        @pl.when(b == 0)
        def _():
            start_gather(0, 0)

        def drain(r, c):
            row_copy(0, r, slot).wait()
            return c

        lax.fori_loop(0, blk, drain, 0, unroll=8)
        xb_ref[...] = xf_ref[slot].astype(BF16)

        @pl.when(b + 1 < n_used)
        def _():
            start_gather(b + 1, 1 - slot)

        y_ref[...] = jnp.broadcast_to(b2_ref[...], y_ref.shape)

    @pl.when(used)
    def _():
        x = xb_ref[...]
        glu = jnp.minimum(_dot(x, w1g_ref[...]) + b1g_ref[...], SWIGLU_LIMIT)
        lin = jnp.clip(_dot(x, w1l_ref[...]) + b1l_ref[...], -SWIGLU_LIMIT, SWIGLU_LIMIT)
        act = (glu * _sigmoid(SWIGLU_ALPHA * glu) * (lin + 1.0)).astype(BF16)
        for j in range(y_ref.shape[1] // tn):
            cs = slice(j * tn, (j + 1) * tn)
            y_ref[:, cs] += _dot(act, w2_ref[:, cs])

    @pl.when(jnp.logical_not(used) & (f == 0))
    def _():
        y_ref[...] = jnp.zeros(y_ref.shape, F32)


def _moe_experts(block_e, n_used, row_tok2, h2, w1g, w1l, b1g, b1l, w2, b2, tf, tn):
    n_blocks, blk = row_tok2.shape
    E, D, F = w1g.shape
    nf = F // tf

    def bb(b, nu):
        return jnp.minimum(b, nu[0] - 1)

    def ff(b, f, nu):
        return jnp.where(b < nu[0], f, nf - 1)

    grid_spec = pltpu.PrefetchScalarGridSpec(
        num_scalar_prefetch=2,
        grid=(n_blocks, nf),
        in_specs=[
            pl.BlockSpec(memory_space=pl.ANY),
            pl.BlockSpec(memory_space=pl.ANY),
            pl.BlockSpec((None, D, tf), lambda b, f, be, nu: (be[bb(b, nu)], 0, ff(b, f, nu))),
            pl.BlockSpec((None, D, tf), lambda b, f, be, nu: (be[bb(b, nu)], 0, ff(b, f, nu))),
            pl.BlockSpec((None, 1, tf), lambda b, f, be, nu: (be[bb(b, nu)], 0, ff(b, f, nu))),
            pl.BlockSpec((None, 1, tf), lambda b, f, be, nu: (be[bb(b, nu)], 0, ff(b, f, nu))),
            pl.BlockSpec((None, tf, D), lambda b, f, be, nu: (be[bb(b, nu)], ff(b, f, nu), 0)),
            pl.BlockSpec((None, 1, D), lambda b, f, be, nu: (be[bb(b, nu)], 0, 0)),
        ],
        out_specs=pl.BlockSpec((blk, D), lambda b, f, be, nu: (b, 0)),
        scratch_shapes=[
            pltpu.SMEM((2, blk), jnp.int32),
            pltpu.VMEM((2, blk, D), F32),
            pltpu.VMEM((blk, D), BF16),
            pltpu.SemaphoreType.DMA,
            pltpu.SemaphoreType.DMA((2,)),
        ],
    )
    return pl.pallas_call(
        functools.partial(_moe_kernel, tn=tn),
        grid_spec=grid_spec,
        out_shape=jax.ShapeDtypeStruct((n_blocks * blk, D), F32),
        compiler_params=_cparams(2),
    )(block_e, n_used, row_tok2, h2, w1g, w1l, b1g, b1l, w2, b2)


def _combine_kernel(pos_hbm, ys_hbm, x1_ref, tg_ref, o_ref, idx_smem, buf_ref, sem_i, sem_g):
    i = pl.program_id(0)
    tc = x1_ref.shape[0]
    n = idx_smem.shape[1]
    slot = i % 2

    def row_copy(src, r, sl):
        return pltpu.make_async_copy(ys_hbm.at[pl.ds(src, 1), :], buf_ref.at[sl, pl.ds(r, 1), :],
                                     sem_g.at[sl])

    def start_gather(step, sl):
        cp = pltpu.make_async_copy(pos_hbm.at[step], idx_smem.at[sl], sem_i)
        cp.start()
        cp.wait()

        def issue(r, c):
            row_copy(idx_smem[sl, r], r, sl).start()
            return c

        lax.fori_loop(0, n, issue, 0, unroll=8)

    @pl.when(i == 0)
    def _():
        start_gather(0, 0)

    @pl.when(i + 1 < pl.num_programs(0))
    def _():
        start_gather(i + 1, 1 - slot)

    def drain(r, c):
        row_copy(0, r, slot).wait()
        return c

    lax.fori_loop(0, n, drain, 0, unroll=8)
    acc = x1_ref[...]
    tg = tg_ref[...]
    for k in range(TOP_K):
        acc += buf_ref[slot, k * tc:(k + 1) * tc, :] * tg[:, k:k + 1]
    o_ref[...] = acc


def _combine(pos_tiles, ys, x1, tg, tc):
    T, D = x1.shape
    return pl.pallas_call(
        _combine_kernel,
        grid=(T // tc,),
        in_specs=[
            pl.BlockSpec(memory_space=pl.ANY),
            pl.BlockSpec(memory_space=pl.ANY),
            pl.BlockSpec((tc, D), lambda i: (i, 0)),
            pl.BlockSpec((tc, 128), lambda i: (i, 0)),
        ],
        out_specs=pl.BlockSpec((tc, D), lambda i: (i, 0)),
        out_shape=jax.ShapeDtypeStruct((T, D), F32),
        scratch_shapes=[
            pltpu.SMEM((2, TOP_K * tc), jnp.int32),
            pltpu.VMEM((2, TOP_K * tc, D), F32),
            pltpu.SemaphoreType.DMA,
            pltpu.SemaphoreType.DMA((2,)),
        ],
        compiler_params=_cparams(1),
    )(pos_tiles, ys, x1, tg)


def _route(top_idx, n_experts, blk):
    T = top_idx.shape[0]
    N = T * TOP_K
    e_flat = top_idx.reshape(N)
    tok_flat = jnp.repeat(jnp.arange(T, dtype=jnp.int32), TOP_K)
    onehot = (e_flat[:, None] == jnp.arange(n_experts, dtype=jnp.int32)[None, :]).astype(jnp.int32)
    csum = jnp.cumsum(onehot, axis=0)
    rank = jnp.sum(csum * onehot, axis=1) - 1
    counts = csum[-1]
    padded = ((counts + blk - 1) // blk) * blk
    pend = jnp.cumsum(padded)
    pstart = pend - padded
    dest = pstart[e_flat] + rank
    n_blocks = -(-(N + n_experts * (blk - 1)) // blk)
    rows = n_blocks * blk
    row_tok = jnp.zeros((rows,), jnp.int32).at[dest].set(tok_flat)
    starts = jnp.arange(n_blocks, dtype=jnp.int32) * blk
    block_e = jnp.minimum(jnp.sum((pend[None, :] <= starts[:, None]).astype(jnp.int32), axis=1),
                          n_experts - 1).astype(jnp.int32)
    n_used = (pend[-1:] // blk).astype(jnp.int32)
    return dest, row_tok.reshape(n_blocks, blk), block_e, n_used


def _cmp_to_sel(nc, n_sel):
    ratio = SEL_BLOCK // CMP_STRIDE
    offs = (np.arange(ratio)[:, None] - np.arange(CMP_LEN // CMP_STRIDE)[None, :]).reshape(-1)
    diff = np.arange(nc)[:, None] - ratio * np.arange(n_sel)[None, :]
    return (diff[..., None] == offs).sum(-1).astype(np.float32)


def _pick(n, pref):
    for c in pref:
        if n % c == 0:
            return c
    return n


def kernel(x, mem, positions, mix_norm_g, mem_norm_g, w_in, w_out, gmlp_ws, gmlp_bs, gmlp_vg,
           nsa_cmp_pos, nsa_ck1, nsa_ck2, nsa_cv1, nsa_cv2, nsa_q_g, nsa_k_g,
           xattn_wkv, xattn_q_g, xattn_k_g, ffn_norm_g, w_router, b_router,
           w_e1, b_e1, w_e2, b_e2):
    B, S, D = x.shape
    T = B * S
    depth = w_in.shape[0]
    n_experts = w_router.shape[-1]
    NC = S // CMP_STRIDE
    n_sel = S // SEL_BLOCK
    tm = _pick(S, (512, 256, 128))
    tk = _pick(S, (512, 256, 128))

    half = HEAD_DIM // 2
    inv = ROPE_THETA ** (-jnp.arange(half, dtype=F32) * 2.0 / HEAD_DIM)
    ang = (positions.astype(F32).reshape(T, 1) * jnp.concatenate([inv, inv])[None, :])
    c2st = jnp.asarray(_cmp_to_sel(NC, n_sel).T, BF16)

    xs = x.reshape(T, D)
    for l in range(depth):
        wi = w_in[l]
        n_gate = 3 * NSA_REP
        zpad = jnp.zeros((D, 128 - n_gate), wi.dtype)
        w_r = jnp.concatenate(
            [wi[:, :3584], wi[:, 3608:4120],
             wi[:, 3584:3584 + n_gate], zpad, wi[:, 3584 + n_gate:3608], zpad], axis=1).astype(BF16)

        proj = _in_proj(xs, mix_norm_g[l][None, :], w_r, tm, PROJ_COLS // 2)

        bias_full = jnp.repeat(jnp.transpose(gmlp_bs[l]), HEAD_DIM, axis=1)
        y_a = _gmlp(proj, gmlp_ws[l], bias_full, gmlp_vg[l].reshape(1, GMLP_WIDTH), T, tm)

        qn, qr, ks, vs, kw, vw = _prep(proj, ang, nsa_q_g[l][None, :], nsa_k_g[l][None, :], B, S, tm)

        w1 = jnp.stack([nsa_ck1[l], nsa_cv1[l]]).astype(BF16)
        w2 = jnp.stack([nsa_ck2[l], nsa_cv2[l]]).astype(BF16)
        cmp_kv = _compress(proj.reshape(B, NC, CMP_STRIDE, PROJ_COLS), w1, w2,
                           nsa_cmp_pos[l], nsa_k_g[l][None, :], B, NC)

        o_c, sel = _cmp_attn(qn, cmp_kv, proj, c2st, B, S, Q_BLOCK)
        o_s = _sel_attn(qr, ks, vs, proj, sel, B, S, _pick(S, (256, 128)), _pick(S, (1024, 512, 256)),
                        _pick(S, (512, 256)))
        o_w = _win_attn(qr, kw, vw, proj, B, S, _pick(S, (256, 128)))

        km, vm = _mem_kv(mem, mem_norm_g[l][None, :], xattn_wkv[l].astype(BF16), xattn_k_g[l][None, :])
        y_c = _xattn(proj, km, vm, xattn_q_g[l][None, :], B, S, tm)

        wr = jnp.pad(w_router[l], ((0, 0), (0, 128 - n_experts)))
        wr_hi = wr.astype(BF16)
        wr_lo = (wr - wr_hi.astype(F32)).astype(BF16)
        br = jnp.pad(b_router[l], (0, 128 - n_experts))[None, :]
        x1, h2, ti, tg = _out_proj(xs, y_a, o_c, o_s, o_w, y_c, w_out[l].astype(BF16),
                                   ffn_norm_g[l][None, :], wr_hi, wr_lo, br, n_experts,
                                   _pick(S, (256, 128)))

        top_idx = ti[:, :TOP_K].astype(jnp.int32)
        dest, row_tok2, block_e, n_used = _route(top_idx, n_experts, MOE_BLOCK)

        d_ff = w_e2.shape[2]
        w1g, w1l = _deinterleave(w_e1[l].reshape(n_experts * D, 2 * d_ff), 512, _pick(2 * d_ff, (2048, 1024, 512, 256)))
        w1g = w1g.reshape(n_experts, D, d_ff)
        w1l = w1l.reshape(n_experts, D, d_ff)
        b1g = b_e1[l][:, None, 0::2]
        b1l = b_e1[l][:, None, 1::2]
        ys = _moe_experts(block_e, n_used, row_tok2, h2, w1g, w1l, b1g, b1l,
                          w_e2[l].astype(BF16), b_e2[l][:, None, :], _pick(d_ff, (1024, 512)),
                          _pick(D, (512, 256)))

        tc = 128
        pos_tiles = dest.reshape(T // tc, tc, TOP_K).transpose(0, 2, 1).reshape(T // tc, TOP_K * tc)
        xs = _combine(pos_tiles, ys, x1, tg, tc)
    return xs.reshape(B, S, D)
```

```python
import functools

import numpy as np
import jax
import jax.numpy as jnp
from jax import lax
from jax.experimental import pallas as pl
from jax.experimental.pallas import tpu as pltpu

F32 = jnp.float32
BF16 = jnp.bfloat16

HEAD_DIM = 128
GMLP_GROUPS = 4
NSA_HEADS = 8
NSA_KV_HEADS = 2
NSA_REP = NSA_HEADS // NSA_KV_HEADS
XATTN_HEADS = 4
GMLP_WIDTH = GMLP_GROUPS * HEAD_DIM
NSA_WIDTH = NSA_HEADS * HEAD_DIM
XATTN_WIDTH = XATTN_HEADS * HEAD_DIM
GMLP_CHUNK = 128
CMP_LEN = 32
CMP_STRIDE = 16
SEL_BLOCK = 64
SEL_TOPK = 16
WINDOW = 512
Q_BLOCK = 128
TOP_K = 4
SWIGLU_ALPHA = 1.702
SWIGLU_LIMIT = 7.0
MOE_BLOCK = 512
ROPE_THETA = 10000.0
EPS = 1e-6
NEG_BIG = -1e30
MASK_BIG = 2.0 ** 100

COL_UV = 0
COL_Q = 8
COL_KV = 16
COL_QX = 28
COL_GATE = 32
N_COL_BLOCKS = 34
PROJ_COLS = N_COL_BLOCKS * 128

VMEM_LIMIT = 56 * 1024 * 1024


def _cparams(n_axes):
    return pltpu.CompilerParams(
        dimension_semantics=("arbitrary",) * n_axes, vmem_limit_bytes=VMEM_LIMIT)


def _rms(x, g):
    ms = jnp.mean(x * x, axis=-1, keepdims=True)
    return x * lax.rsqrt(ms + EPS) * g


def _gelu_tanh(x):
    c = np.float32(np.sqrt(2.0 / np.pi))
    return 0.5 * x * (1.0 + jnp.tanh(c * (x + 0.044715 * (x * x * x))))


def _sigmoid(x):
    return 1.0 / (1.0 + jnp.exp(-x))


def _dot(a, b):
    return jnp.dot(a, b, preferred_element_type=F32)


def _dot_nt(a, b):
    return lax.dot_general(a, b, (((1,), (1,)), ((), ())), preferred_element_type=F32)


def _inproj_kernel(x_ref, g_ref, w_ref, o_ref, h_ref):
    @pl.when(pl.program_id(1) == 0)
    def _():
        h_ref[...] = _rms(x_ref[...], g_ref[...]).astype(BF16)

    o_ref[...] = _dot(h_ref[...], w_ref[...])


def _in_proj(x2, g, w_r, tm, tn):
    T, D = x2.shape
    NC = w_r.shape[1]
    return pl.pallas_call(
        _inproj_kernel,
        grid=(T // tm, NC // tn),
        in_specs=[
            pl.BlockSpec((tm, D), lambda i, j: (i, 0)),
            pl.BlockSpec((1, D), lambda i, j: (0, 0)),
            pl.BlockSpec((D, tn), lambda i, j: (0, j)),
        ],
        out_specs=pl.BlockSpec((tm, tn), lambda i, j: (i, j)),
        out_shape=jax.ShapeDtypeStruct((T, NC), F32),
        scratch_shapes=[pltpu.VMEM((tm, D), BF16)],
        compiler_params=_cparams(2),
    )(x2, g, w_r)


def _prep_kernel(q_ref, sel_ref, win_ref, ang_ref, qg_ref, kg_ref,
                 qn_o, qr_o, ks_o, vs_o, kw_o, vw_o):
    ang = ang_ref[...]
    lane = lax.broadcasted_iota(jnp.int32, (1, HEAD_DIM), 1)
    cosf = jnp.cos(ang)
    sinf = jnp.sin(ang) * jnp.where(lane < HEAD_DIM // 2, -1.0, 1.0)
    scale = np.float32(HEAD_DIM ** -0.5 * np.log2(np.e))

    def rope(x):
        return x * cosf + pltpu.roll(x, HEAD_DIM // 2, axis=1) * sinf

    qg = qg_ref[...]
    kg = kg_ref[...]
    for h in range(NSA_HEADS):
        sl = slice(h * HEAD_DIM, (h + 1) * HEAD_DIM)
        n = _rms(q_ref[:, sl], qg)
        qn_o[:, sl] = (n * scale).astype(BF16)
        qr_o[:, sl] = (rope(n) * scale).astype(BF16)
    for g in range(NSA_KV_HEADS):
        sl = slice(g * HEAD_DIM, (g + 1) * HEAD_DIM)
        sv = slice((NSA_KV_HEADS + g) * HEAD_DIM, (NSA_KV_HEADS + g + 1) * HEAD_DIM)
        ks_o[g] = rope(_rms(sel_ref[:, sl], kg)).astype(BF16)
        vs_o[g] = sel_ref[:, sv].astype(BF16)
        kw_o[g] = rope(_rms(win_ref[:, sl], kg)).astype(BF16)
        vw_o[g] = win_ref[:, sv].astype(BF16)


def _prep(proj, ang, q_g, k_g, B, S, tm):
    T = B * S
    nsb = S // tm
    kv_shape = jax.ShapeDtypeStruct((B, NSA_KV_HEADS, S, HEAD_DIM), BF16)
    kv_spec = pl.BlockSpec((None, NSA_KV_HEADS, tm, HEAD_DIM),
                           lambda i: (i // nsb, 0, i % nsb, 0))
    return pl.pallas_call(
        _prep_kernel,
        grid=(T // tm,),
        in_specs=[
            pl.BlockSpec((tm, NSA_WIDTH), lambda i: (i, COL_Q * 128 // NSA_WIDTH)),
            pl.BlockSpec((tm, 512), lambda i: (i, (COL_KV + 4) * 128 // 512)),
            pl.BlockSpec((tm, 512), lambda i: (i, (COL_KV + 8) * 128 // 512)),
            pl.BlockSpec((tm, HEAD_DIM), lambda i: (i, 0)),
            pl.BlockSpec((1, HEAD_DIM), lambda i: (0, 0)),
            pl.BlockSpec((1, HEAD_DIM), lambda i: (0, 0)),
        ],
        out_specs=[
            pl.BlockSpec((tm, NSA_WIDTH), lambda i: (i, 0)),
            pl.BlockSpec((tm, NSA_WIDTH), lambda i: (i, 0)),
            kv_spec, kv_spec, kv_spec, kv_spec,
        ],
        out_shape=[
            jax.ShapeDtypeStruct((T, NSA_WIDTH), BF16),
            jax.ShapeDtypeStruct((T, NSA_WIDTH), BF16),
            kv_shape, kv_shape, kv_shape, kv_shape,
        ],
        compiler_params=_cparams(1),
    )(proj, proj, proj, ang, q_g, k_g)


def _gmlp_kernel(uv_ref, ws_ref, bias_ref, gv_ref, o_ref, *, n_chunks):
    row = lax.broadcasted_iota(jnp.int32, (GMLP_CHUNK, GMLP_CHUNK), 0)
    col = lax.broadcasted_iota(jnp.int32, (GMLP_CHUNK, GMLP_CHUNK), 1)
    causal = col <= row
    for g in range(GMLP_GROUPS):
        sl = slice(g * HEAD_DIM, (g + 1) * HEAD_DIM)
        sv = slice(GMLP_WIDTH + g * HEAD_DIM, GMLP_WIDTH + (g + 1) * HEAD_DIM)
        w = jnp.where(causal, ws_ref[g], 0.0).astype(BF16)
        u = _gelu_tanh(uv_ref[:, sl])
        v = _rms(_gelu_tanh(uv_ref[:, sv]), gv_ref[:, sl]).astype(BF16)
        bias = bias_ref[:, sl]
        for c in range(n_chunks):
            rs = slice(c * GMLP_CHUNK, (c + 1) * GMLP_CHUNK)
            vs = _dot(w, v[rs]) + bias
            o_ref[rs, sl] = (u[rs] * vs).astype(BF16)


def _gmlp(proj, ws, bias_full, gv, T, tm):
    return pl.pallas_call(
        functools.partial(_gmlp_kernel, n_chunks=tm // GMLP_CHUNK),
        grid=(T // tm,),
        in_specs=[
            pl.BlockSpec((tm, 2 * GMLP_WIDTH), lambda i: (i, 0)),
            pl.BlockSpec((GMLP_GROUPS, GMLP_CHUNK, GMLP_CHUNK), lambda i: (0, 0, 0)),
            pl.BlockSpec((GMLP_CHUNK, GMLP_WIDTH), lambda i: (0, 0)),
            pl.BlockSpec((1, GMLP_WIDTH), lambda i: (0, 0)),
        ],
        out_specs=pl.BlockSpec((tm, GMLP_WIDTH), lambda i: (i, 0)),
        out_shape=jax.ShapeDtypeStruct((T, GMLP_WIDTH), BF16),
        compiler_params=_cparams(1),
    )(proj, ws, bias_full, gv)


def _compress_kernel(x_ref, w1_ref, w2_ref, pos_ref, kg_ref, o_ref):
    nc = x_ref.shape[0]
    kind = pl.program_id(1)
    acc_a = jnp.zeros((nc, HEAD_DIM), F32)
    acc_b = jnp.zeros((nc, HEAD_DIM), F32)
    for j in range(CMP_STRIDE):
        xj = x_ref[:, j, :]
        wa = w1_ref[j * HEAD_DIM:(j + 1) * HEAD_DIM, :]
        wb = w1_ref[(CMP_STRIDE + j) * HEAD_DIM:(CMP_STRIDE + j + 1) * HEAD_DIM, :]
        acc_a += _dot((xj + pos_ref[j:j + 1, :]).astype(BF16), wa)
        acc_b += _dot((xj + pos_ref[CMP_STRIDE + j:CMP_STRIDE + j + 1, :]).astype(BF16), wb)
    hidden = acc_a + pltpu.roll(acc_b, nc - 1, axis=0)
    out = _dot(_gelu_tanh(hidden).astype(BF16), w2_ref[...])
    o_ref[...] = jnp.where(kind == 0, _rms(out, kg_ref[...]), out).astype(BF16)


def _compress(proj4, w1, w2, cmp_pos, k_g, B, NC):
    return pl.pallas_call(
        _compress_kernel,
        grid=(B, 2, NSA_KV_HEADS),
        in_specs=[
            pl.BlockSpec((None, NC, CMP_STRIDE, HEAD_DIM),
                         lambda b, k, g: (b, 0, 0, COL_KV + 2 * k + g)),
            pl.BlockSpec((None, CMP_LEN * HEAD_DIM, HEAD_DIM), lambda b, k, g: (k, 0, 0)),
            pl.BlockSpec((None, HEAD_DIM, HEAD_DIM), lambda b, k, g: (k, 0, 0)),
            pl.BlockSpec((CMP_LEN, HEAD_DIM), lambda b, k, g: (0, 0)),
            pl.BlockSpec((1, HEAD_DIM), lambda b, k, g: (0, 0)),
        ],
        out_specs=pl.BlockSpec((None, None, None, NC, HEAD_DIM), lambda b, k, g: (b, k, g, 0, 0)),
        out_shape=jax.ShapeDtypeStruct((B, 2, NSA_KV_HEADS, NC, HEAD_DIM), BF16),
        compiler_params=_cparams(3),
    )(proj4, w1, w2, cmp_pos, k_g)


def _cmp_attn_kernel(q_ref, kc_ref, vc_ref, gl_ref, c2st_ref, oc_ref, sel_ref, q_sc, imp_sc,
                     *, k_top, widths):
    i = pl.program_id(2)
    tq = q_ref.shape[0]
    n_sel = c2st_ref.shape[0]
    t = i * tq + lax.broadcasted_iota(jnp.int32, (tq, 1), 0)
    gl = gl_ref[...]
    for r in range(NSA_REP):
        q_sc[r * tq:(r + 1) * tq, :] = q_ref[:, r * HEAD_DIM:(r + 1) * HEAD_DIM]

    def attend(nc):
        cmp_end = lax.broadcasted_iota(jnp.int32, (1, nc), 1) * CMP_STRIDE + (CMP_LEN - 1)
        bias = jnp.where(cmp_end <= t, 0.0, -jnp.inf)
        s = _dot_nt(q_sc[...], kc_ref[0:nc, :]) + jnp.concatenate([bias] * NSA_REP, axis=0)
        m = jnp.max(s, axis=-1, keepdims=True)
        m = jnp.where(m > -jnp.inf, m, 0.0)
        e = jnp.exp2(s - m)
        d = jnp.sum(e, axis=-1, keepdims=True)
        p = e * (1.0 / jnp.where(d > 0, d, 1.0))
        o = _dot(p.astype(BF16), vc_ref[0:nc, :])
        pc_sum = jnp.zeros((tq, nc), F32)
        for r in range(NSA_REP):
            rs = slice(r * tq, (r + 1) * tq)
            gate = _sigmoid(gl[:, 3 * r:3 * r + 1])
            oc_ref[:, r * HEAD_DIM:(r + 1) * HEAD_DIM] = o[rs, :] * gate
            pc_sum += p[rs, :]
        hi = pc_sum.astype(BF16)
        lo = (pc_sum - hi.astype(F32)).astype(BF16)
        imp_sc[...] = _dot_nt(c2st_ref[:, 0:nc], hi) + _dot_nt(c2st_ref[:, 0:nc], lo)

    needed = (i * tq + tq - CMP_LEN) // CMP_STRIDE + 1
    for j, w in enumerate(widths):
        conds = ([needed <= w] if j + 1 < len(widths) else []) + ([needed > widths[j - 1]] if j else [])
        if conds:
            pl.when(functools.reduce(jnp.logical_and, conds))(functools.partial(attend, w))
        else:
            attend(w)

    imp = imp_sc[...]
    sid = lax.broadcasted_iota(jnp.int32, (n_sel, 1), 0)
    sid_f = sid.astype(F32)
    tt = i * tq + lax.broadcasted_iota(jnp.int32, (1, tq), 1)
    cur = tt // SEL_BLOCK
    valid = sid * SEL_BLOCK <= tt
    forced = (sid == 0) | (sid == cur) | (sid == cur - 1)
    score = jnp.where(valid, jnp.where(forced, jnp.inf, imp), -jnp.inf)
    sel = jnp.zeros((n_sel, tq), F32)
    for _ in range(k_top):
        m = jnp.max(score, axis=0, keepdims=True)
        idx = jnp.min(jnp.where(score == m, sid_f, np.float32(n_sel)), axis=0, keepdims=True)
        pick = sid_f == idx
        sel = jnp.where(pick, 1.0, sel)
        score = jnp.where(pick, -jnp.inf, score)
    selm = jnp.transpose(sel - 1.0)
    pad = sel_ref.shape[1] - n_sel
    if pad:
        selm = jnp.concatenate([selm, jnp.full((tq, pad), -1.0, F32)], axis=1)
    sel_ref[...] = selm


def _cmp_attn(qn, cmp_kv, proj, c2st, B, S, tq):
    T = B * S
    nqb = S // tq
    NC = cmp_kv.shape[3]
    n_sel = S // SEL_BLOCK
    n_sel_pad = -(-n_sel // 128) * 128
    gw = NSA_REP * HEAD_DIM
    return pl.pallas_call(
        functools.partial(_cmp_attn_kernel, k_top=min(SEL_TOPK, n_sel),
                          widths=tuple(NC * (j + 1) // 4 for j in range(4)) if NC % 1024 == 0 else (NC,)),
        grid=(B, NSA_KV_HEADS, nqb),
        in_specs=[
            pl.BlockSpec((tq, gw), lambda b, g, i: (b * nqb + i, g)),
            pl.BlockSpec((None, None, None, NC, HEAD_DIM), lambda b, g, i: (b, 0, g, 0, 0)),
            pl.BlockSpec((None, None, None, NC, HEAD_DIM), lambda b, g, i: (b, 1, g, 0, 0)),
            pl.BlockSpec((tq, 128), lambda b, g, i: (b * nqb + i, COL_GATE + g)),
            pl.BlockSpec((n_sel, NC), lambda b, g, i: (0, 0)),
        ],
        out_specs=[
            pl.BlockSpec((tq, gw), lambda b, g, i: (b * nqb + i, g)),
            pl.BlockSpec((None, None, tq, n_sel_pad), lambda b, g, i: (b, g, i, 0)),
        ],
        out_shape=[
            jax.ShapeDtypeStruct((T, NSA_WIDTH), F32),
            jax.ShapeDtypeStruct((B, NSA_KV_HEADS, S, n_sel_pad), F32),
        ],
        scratch_shapes=[pltpu.VMEM((NSA_REP * tq, HEAD_DIM), BF16), pltpu.VMEM((n_sel, tq), F32)],
        compiler_params=_cparams(3),
    )(qn, cmp_kv, cmp_kv, proj, c2st)


def _sel_attn_kernel(q_ref, k_ref, v_ref, gl_ref, sel_ref, e0_ref, o_ref,
                     q_sc, m_sc, l_sc, acc_sc, *, tk, tkd):
    i = pl.program_id(2)
    tq = q_ref.shape[0]
    s0 = i * tq
    t = s0 + lax.broadcasted_iota(jnp.int32, (tq, 1), 0)
    for r in range(NSA_REP):
        q_sc[r * tq:(r + 1) * tq, :] = q_ref[:, r * HEAD_DIM:(r + 1) * HEAD_DIM]
    m_sc[...] = jnp.full(m_sc.shape, NEG_BIG, F32)
    l_sc[...] = jnp.zeros(l_sc.shape, F32)
    acc_sc[...] = jnp.zeros(acc_sc.shape, F32)
    def step(koff, width, diagonal):
        koff = pl.multiple_of(koff, width)
        k = k_ref[pl.ds(koff, width), :]
        v = v_ref[pl.ds(koff, width), :]
        c0 = koff // SEL_BLOCK
        slab = sel_ref[:, pl.ds(pl.multiple_of((c0 // 128) * 128, 128), 128)]
        rolled = pltpu.roll(slab, (128 - c0 % 128) % 128, axis=1)
        bias = _dot(rolled.astype(BF16), e0_ref[:, :width])
        if diagonal:
            key = koff + lax.broadcasted_iota(jnp.int32, (1, width), 1)
            bias = jnp.where(key <= t, bias, -MASK_BIG)
        s = _dot_nt(q_sc[...], k) + jnp.concatenate([bias] * NSA_REP, axis=0)
        m_old = m_sc[...]
        m_new = jnp.maximum(m_old, jnp.max(s, axis=-1, keepdims=True))
        alpha = jnp.exp2(m_old - m_new)
        p = jnp.exp2(s - jnp.concatenate([m_new] * (width // 128), axis=1))
        l_sc[...] = alpha * l_sc[...] + jnp.sum(p, axis=-1, keepdims=True)
        acc_sc[...] = alpha * acc_sc[...] + _dot(p.astype(BF16), v)
        m_sc[...] = m_new

    def body(j, carry):
        step(j * tk, tk, False)
        return carry

    n_wide = s0 // tk
    lax.fori_loop(0, n_wide, body, 0)
    tail = n_wide * tk
    n_narrow = (s0 - tail) // tkd
    for u in range(tk // tkd - 1):
        @pl.when(u < n_narrow)
        def _():
            step(tail + u * tkd, tkd, False)

    step(tail + n_narrow * tkd, tkd, True)
    gl = gl_ref[...]
    for r in range(NSA_REP):
        rs = slice(r * tq, (r + 1) * tq)
        gate = _sigmoid(gl[:, 3 * r + 1:3 * r + 2])
        o_ref[:, r * HEAD_DIM:(r + 1) * HEAD_DIM] = acc_sc[rs, :] / l_sc[rs, :] * gate


def _sel_attn(qr, ks, vs, proj, sel, B, S, tq, tk, tkd):
    assert tk % tkd == 0 and tkd % tq == 0 and tkd % SEL_BLOCK == 0 and tk // SEL_BLOCK <= 128
    T = B * S
    nqb = S // tq
    n_sel = S // SEL_BLOCK
    gw = NSA_REP * HEAD_DIM
    n_sel_pad = sel.shape[-1]
    e0 = np.where(np.arange(128)[:, None] == np.arange(tk)[None, :] // SEL_BLOCK, MASK_BIG, 0.0)
    return pl.pallas_call(
        functools.partial(_sel_attn_kernel, tk=tk, tkd=tkd),
        grid=(B, NSA_KV_HEADS, nqb),
        in_specs=[
            pl.BlockSpec((tq, gw), lambda b, g, i: (b * nqb + i, g)),
            pl.BlockSpec((None, None, S, HEAD_DIM), lambda b, g, i: (b, g, 0, 0)),
            pl.BlockSpec((None, None, S, HEAD_DIM), lambda b, g, i: (b, g, 0, 0)),
            pl.BlockSpec((tq, 128), lambda b, g, i: (b * nqb + i, COL_GATE + g)),
            pl.BlockSpec((None, None, tq, n_sel_pad), lambda b, g, i: (b, g, i, 0)),
            pl.BlockSpec((128, tk), lambda b, g, i: (0, 0)),
        ],
        out_specs=pl.BlockSpec((tq, gw), lambda b, g, i: (b * nqb + i, g)),
        out_shape=jax.ShapeDtypeStruct((T, NSA_WIDTH), F32),
        scratch_shapes=[
            pltpu.VMEM((NSA_REP * tq, HEAD_DIM), BF16),
            pltpu.VMEM((NSA_REP * tq, 128), F32),
            pltpu.VMEM((NSA_REP * tq, 128), F32),
            pltpu.VMEM((NSA_REP * tq, HEAD_DIM), F32),
        ],
        compiler_params=_cparams(3),
    )(qr, ks, vs, proj, sel, jnp.asarray(e0, BF16))


def _win_attn_kernel(q_ref, k_ref, v_ref, gl_ref, o_ref, q_sc, *, wk):
    i = pl.program_id(2)
    tq = q_ref.shape[0]
    s0 = i * tq
    t = s0 + lax.broadcasted_iota(jnp.int32, (tq, 1), 0)
    for r in range(NSA_REP):
        q_sc[r * tq:(r + 1) * tq, :] = q_ref[:, r * HEAD_DIM:(r + 1) * HEAD_DIM]
    ws = pl.multiple_of(jnp.maximum(s0 - WINDOW, 0), tq)
    k = k_ref[pl.ds(ws, wk), :]
    v = v_ref[pl.ds(ws, wk), :]
    kp = ws + lax.broadcasted_iota(jnp.int32, (1, wk), 1)
    bias = jnp.where(kp <= t, jnp.where(kp > t - WINDOW, 0.0, NEG_BIG), NEG_BIG)
    s = _dot_nt(q_sc[...], k) + jnp.concatenate([bias] * NSA_REP, axis=0)
    m = jnp.max(s, axis=-1, keepdims=True)
    e = jnp.exp2(s - m)
    p = e * (1.0 / jnp.sum(e, axis=-1, keepdims=True))
    o = _dot(p.astype(BF16), v)
    gl = gl_ref[...]
    for r in range(NSA_REP):
        gate = _sigmoid(gl[:, 3 * r + 2:3 * r + 3])
        o_ref[:, r * HEAD_DIM:(r + 1) * HEAD_DIM] = o[r * tq:(r + 1) * tq, :] * gate


def _win_attn(qr, kw, vw, proj, B, S, tq):
    T = B * S
    nqb = S // tq
    gw = NSA_REP * HEAD_DIM
    wk = min(WINDOW + tq, S)
    return pl.pallas_call(
        functools.partial(_win_attn_kernel, wk=wk),
        grid=(B, NSA_KV_HEADS, nqb),
        in_specs=[
            pl.BlockSpec((tq, gw), lambda b, g, i: (b * nqb + i, g)),
            pl.BlockSpec((None, None, S, HEAD_DIM), lambda b, g, i: (b, g, 0, 0)),
            pl.BlockSpec((None, None, S, HEAD_DIM), lambda b, g, i: (b, g, 0, 0)),
            pl.BlockSpec((tq, 128), lambda b, g, i: (b * nqb + i, COL_GATE + g)),
        ],
        out_specs=pl.BlockSpec((tq, gw), lambda b, g, i: (b * nqb + i, g)),
        out_shape=jax.ShapeDtypeStruct((T, NSA_WIDTH), F32),
        scratch_shapes=[pltpu.VMEM((NSA_REP * tq, HEAD_DIM), BF16)],
        compiler_params=_cparams(3),
    )(qr, kw, vw, proj)


def _memkv_kernel(mem_ref, g_ref, w_ref, kg_ref, k_o, v_o):
    h = _rms(mem_ref[...], g_ref[...]).astype(BF16)
    kv = _dot(h, w_ref[...])
    kg = kg_ref[...]
    for hd in range(XATTN_HEADS):
        sl = slice(hd * HEAD_DIM, (hd + 1) * HEAD_DIM)
        sv = slice(XATTN_WIDTH + hd * HEAD_DIM, XATTN_WIDTH + (hd + 1) * HEAD_DIM)
        k_o[:, sl] = _rms(kv[:, sl], kg).astype(BF16)
        v_o[:, sl] = kv[:, sv].astype(BF16)


def _mem_kv(mem, g, wkv, k_g):
    B, M, D = mem.shape
    shp = jax.ShapeDtypeStruct((B, M, XATTN_WIDTH), BF16)
    spec = pl.BlockSpec((None, M, XATTN_WIDTH), lambda b: (b, 0, 0))
    return pl.pallas_call(
        _memkv_kernel,
        grid=(B,),
        in_specs=[
            pl.BlockSpec((None, M, D), lambda b: (b, 0, 0)),
            pl.BlockSpec((1, D), lambda b: (0, 0)),
            pl.BlockSpec((D, 2 * XATTN_WIDTH), lambda b: (0, 0)),
            pl.BlockSpec((1, HEAD_DIM), lambda b: (0, 0)),
        ],
        out_specs=[spec, spec],
        out_shape=[shp, shp],
        compiler_params=_cparams(1),
    )(mem, g, wkv, k_g)


def _xattn_kernel(q_ref, k_ref, v_ref, qg_ref, o_ref):
    scale = np.float32(HEAD_DIM ** -0.5)
    qg = qg_ref[...]
    for hd in range(XATTN_HEADS):
        sl = slice(hd * HEAD_DIM, (hd + 1) * HEAD_DIM)
        q = (_rms(q_ref[:, sl], qg) * scale).astype(BF16)
        s = _dot_nt(q, k_ref[:, sl])
        m = jnp.max(s, axis=-1, keepdims=True)
        e = jnp.exp(s - m)
        p = e * (1.0 / jnp.sum(e, axis=-1, keepdims=True))
        o_ref[:, sl] = _dot(p.astype(BF16), v_ref[:, sl]).astype(BF16)


def _xattn(proj, km, vm, q_g, B, S, tm):
    T = B * S
    nsb = S // tm
    M = km.shape[1]
    return pl.pallas_call(
        _xattn_kernel,
        grid=(T // tm,),
        in_specs=[
            pl.BlockSpec((tm, XATTN_WIDTH), lambda i: (i, COL_QX * 128 // XATTN_WIDTH)),
            pl.BlockSpec((None, M, XATTN_WIDTH), lambda i: (i // nsb, 0, 0)),
            pl.BlockSpec((None, M, XATTN_WIDTH), lambda i: (i // nsb, 0, 0)),
            pl.BlockSpec((1, HEAD_DIM), lambda i: (0, 0)),
        ],
        out_specs=pl.BlockSpec((tm, XATTN_WIDTH), lambda i: (i, 0)),
        out_shape=jax.ShapeDtypeStruct((T, XATTN_WIDTH), BF16),
        compiler_params=_cparams(1),
    )(proj, km, vm, q_g)


def _outproj_kernel(x_ref, ya_ref, oc_ref, os_ref, ow_ref, yc_ref, wo_ref, g_ref,
                    wrh_ref, wrl_ref, br_ref, x1_o, h2_o, ti_o, tg_o, *, n_experts):
    yb = (oc_ref[...] + os_ref[...] + ow_ref[...]).astype(BF16)
    x1 = x_ref[...]
    x1 += _dot(ya_ref[...], wo_ref[0:GMLP_WIDTH, :])
    x1 += _dot(yb, wo_ref[GMLP_WIDTH:GMLP_WIDTH + NSA_WIDTH, :])
    x1 += _dot(yc_ref[...], wo_ref[GMLP_WIDTH + NSA_WIDTH:, :])
    x1_o[...] = x1
    h2 = _rms(x1, g_ref[...])
    h2_o[...] = h2
    hi = h2.astype(BF16)
    lo = (h2 - hi.astype(F32)).astype(BF16)
    logits = (_dot(hi, wrh_ref[...]) + _dot(lo, wrh_ref[...]) + _dot(hi, wrl_ref[...])
              + br_ref[...])
    lane = lax.broadcasted_iota(jnp.int32, (1, 128), 1)
    lane_f = lane.astype(F32)
    lg = jnp.where(lane < n_experts, logits, -jnp.inf)
    ti = jnp.zeros(logits.shape, F32)
    tg = jnp.zeros(logits.shape, F32)
    denom = jnp.zeros((logits.shape[0], 1), F32)
    v0 = None
    for k in range(TOP_K):
        m = jnp.max(lg, axis=-1, keepdims=True)
        ix = jnp.min(jnp.where(lg == m, lane_f, 128.0), axis=-1, keepdims=True)
        if v0 is None:
            v0 = m
        e = jnp.exp(m - v0)
        denom += e
        ti = jnp.where(lane == k, ix, ti)
        tg = jnp.where(lane == k, e, tg)
        lg = jnp.where(lane_f == ix, -jnp.inf, lg)
    ti_o[...] = ti
    tg_o[...] = tg / denom


def _out_proj(x2, ya, oc, os_, ow, yc, wo, g, wr_hi, wr_lo, br, n_experts, tm):
    T, D = x2.shape
    row = lambda w: pl.BlockSpec((tm, w), lambda i: (i, 0))
    full = lambda a: pl.BlockSpec(a.shape, lambda i: (0,) * a.ndim)
    return pl.pallas_call(
        functools.partial(_outproj_kernel, n_experts=n_experts),
        grid=(T // tm,),
        in_specs=[row(D), row(GMLP_WIDTH), row(NSA_WIDTH), row(NSA_WIDTH), row(NSA_WIDTH),
                  row(XATTN_WIDTH), full(wo), full(g), full(wr_hi), full(wr_lo), full(br)],
        out_specs=[row(D), row(D), row(128), row(128)],
        out_shape=[
            jax.ShapeDtypeStruct((T, D), F32),
            jax.ShapeDtypeStruct((T, D), F32),
            jax.ShapeDtypeStruct((T, 128), F32),
            jax.ShapeDtypeStruct((T, 128), F32),
        ],
        compiler_params=_cparams(1),
    )(x2, ya, oc, os_, ow, yc, wo, g, wr_hi, wr_lo, br)


def _deinterleave_kernel(w_ref, p_ref, g_o, l_o):
    perm = p_ref[...]
    for c in range(w_ref.shape[1] // 256):
        t = _dot(w_ref[:, c * 256:(c + 1) * 256].astype(BF16), perm)
        g_o[:, c * 128:(c + 1) * 128] = t[:, :128].astype(BF16)
        l_o[:, c * 128:(c + 1) * 128] = t[:, 128:].astype(BF16)


def _deinterleave(w, tr, tcw):
    R, C = w.shape
    src = np.arange(256)
    dst = np.where(src % 2 == 0, src // 2, 128 + src // 2)
    perm = np.zeros((256, 256), np.float32)
    perm[src, dst] = 1.0
    out = jax.ShapeDtypeStruct((R, C // 2), BF16)
    return pl.pallas_call(
        _deinterleave_kernel,
        grid=(R // tr, C // tcw),
        in_specs=[
            pl.BlockSpec((tr, tcw), lambda i, j: (i, j)),
            pl.BlockSpec((256, 256), lambda i, j: (0, 0)),
        ],
        out_specs=[pl.BlockSpec((tr, tcw // 2), lambda i, j: (i, j))] * 2,
        out_shape=[out, out],
        compiler_params=_cparams(2),
    )(w, jnp.asarray(perm, BF16))


def _moe_kernel(be_ref, nu_ref, rt_hbm, h2_hbm, w1g_ref, w1l_ref, b1g_ref, b1l_ref,
                w2_ref, b2_ref, y_ref, idx_smem, xf_ref, xb_ref, sem_i, sem_g, *, tn):
    b = pl.program_id(0)
    f = pl.program_id(1)
    blk = xb_ref.shape[0]
    n_used = nu_ref[0]
    used = b < n_used
    slot = b % 2

    def row_copy(tok, g, u, sl):
        return pltpu.make_async_copy(h2_hbm.at[pl.ds(tok, 1), :], xf_ref.at[sl, g, pl.ds(u, 1), :],
                                     sem_g.at[sl])

    def start_gather(blk_id, sl):
        cp = pltpu.make_async_copy(rt_hbm.at[blk_id], idx_smem.at[sl], sem_i)
        cp.start()
        cp.wait()

        def issue(g, c):
            for u in range(8):
                row_copy(idx_smem[sl, g * 8 + u], g, u, sl).start()
            return c

        lax.fori_loop(0, blk // 8, issue, 0)

    @pl.when(used & (f == 0))
    def _():
        @pl.when(b == 0)
        def _():
            start_gather(0, 0)

        def drain(g, c):
            for u in range(8):
                row_copy(0, g, u, slot).wait()
            return c

        lax.fori_loop(0, blk // 8, drain, 0)
        xb_ref[...] = xf_ref[slot].reshape(blk, xb_ref.shape[1]).astype(BF16)

        @pl.when(b + 1 < n_used)
        def _():
            start_gather(b + 1, 1 - slot)

        y_ref[...] = jnp.broadcast_to(b2_ref[...], y_ref.shape)

    @pl.when(used)
    def _():
        x = xb_ref[...]
        glu = jnp.minimum(_dot(x, w1g_ref[...]) + b1g_ref[...], SWIGLU_LIMIT)
        lin = jnp.clip(_dot(x, w1l_ref[...]) + b1l_ref[...], -SWIGLU_LIMIT, SWIGLU_LIMIT)
        act = (glu * _sigmoid(SWIGLU_ALPHA * glu) * (lin + 1.0)).astype(BF16)
        for j in range(y_ref.shape[1] // tn):
            cs = slice(j * tn, (j + 1) * tn)
            y_ref[:, cs] += _dot(act, w2_ref[:, cs])

    @pl.when(jnp.logical_not(used) & (f == 0))
    def _():
        y_ref[...] = jnp.zeros(y_ref.shape, F32)


def _moe_experts(block_e, n_used, row_tok2, h2, w1g, w1l, b1g, b1l, w2, b2, tf, tn):
    n_blocks, blk = row_tok2.shape
    E, D, F = w1g.shape
    nf = F // tf

    def bb(b, nu):
        return jnp.minimum(b, nu[0] - 1)

    def ff(b, f, nu):
        return jnp.where(b < nu[0], f, nf - 1)

    grid_spec = pltpu.PrefetchScalarGridSpec(
        num_scalar_prefetch=2,
        grid=(n_blocks, nf),
        in_specs=[
            pl.BlockSpec(memory_space=pl.ANY),
            pl.BlockSpec(memory_space=pl.ANY),
            pl.BlockSpec((None, D, tf), lambda b, f, be, nu: (be[bb(b, nu)], 0, ff(b, f, nu))),
            pl.BlockSpec((None, D, tf), lambda b, f, be, nu: (be[bb(b, nu)], 0, ff(b, f, nu))),
            pl.BlockSpec((None, 1, tf), lambda b, f, be, nu: (be[bb(b, nu)], 0, ff(b, f, nu))),
            pl.BlockSpec((None, 1, tf), lambda b, f, be, nu: (be[bb(b, nu)], 0, ff(b, f, nu))),
            pl.BlockSpec((None, tf, D), lambda b, f, be, nu: (be[bb(b, nu)], ff(b, f, nu), 0)),
            pl.BlockSpec((None, 1, D), lambda b, f, be, nu: (be[bb(b, nu)], 0, 0)),
        ],
        out_specs=pl.BlockSpec((blk, D), lambda b, f, be, nu: (b, 0)),
        scratch_shapes=[
            pltpu.SMEM((2, blk), jnp.int32),
            pltpu.VMEM((2, blk // 8, 8, D), F32),
            pltpu.VMEM((blk, D), BF16),
            pltpu.SemaphoreType.DMA,
            pltpu.SemaphoreType.DMA((2,)),
        ],
    )
    return pl.pallas_call(
        functools.partial(_moe_kernel, tn=tn),
        grid_spec=grid_spec,
        out_shape=jax.ShapeDtypeStruct((n_blocks * blk, D), F32),
        compiler_params=_cparams(2),
    )(block_e, n_used, row_tok2, h2, w1g, w1l, b1g, b1l, w2, b2)


def _combine_kernel(pos_hbm, ys_hbm, x1_ref, tg_ref, o_ref, idx_smem, buf_ref, sem_i, sem_g):
    i = pl.program_id(0)
    tc = x1_ref.shape[0]
    n = idx_smem.shape[1]
    slot = i % 2

    def row_copy(src, g, u, sl):
        return pltpu.make_async_copy(ys_hbm.at[pl.ds(src, 1), :], buf_ref.at[sl, g, pl.ds(u, 1), :],
                                     sem_g.at[sl])

    def start_gather(step, sl):
        cp = pltpu.make_async_copy(pos_hbm.at[step], idx_smem.at[sl], sem_i)
        cp.start()
        cp.wait()

        def issue(g, c):
            for u in range(8):
                row_copy(idx_smem[sl, g * 8 + u], g, u, sl).start()
            return c

        lax.fori_loop(0, n // 8, issue, 0)

    @pl.when(i == 0)
    def _():
        start_gather(0, 0)

    @pl.when(i + 1 < pl.num_programs(0))
    def _():
        start_gather(i + 1, 1 - slot)

    def drain(g, c):
        for u in range(8):
            row_copy(0, g, u, slot).wait()
        return c

    lax.fori_loop(0, n // 8, drain, 0)
    acc = x1_ref[...]
    tg = tg_ref[...]
    gk = tc // 8
    for k in range(TOP_K):
        rows = buf_ref[slot, k * gk:(k + 1) * gk].reshape(tc, acc.shape[1])
        acc += rows * tg[:, k:k + 1]
    o_ref[...] = acc


def _combine(pos_tiles, ys, x1, tg, tc):
    T, D = x1.shape
    return pl.pallas_call(
        _combine_kernel,
        grid=(T // tc,),
        in_specs=[
            pl.BlockSpec(memory_space=pl.ANY),
            pl.BlockSpec(memory_space=pl.ANY),
            pl.BlockSpec((tc, D), lambda i: (i, 0)),
            pl.BlockSpec((tc, 128), lambda i: (i, 0)),
        ],
        out_specs=pl.BlockSpec((tc, D), lambda i: (i, 0)),
        out_shape=jax.ShapeDtypeStruct((T, D), F32),
        scratch_shapes=[
            pltpu.SMEM((2, TOP_K * tc), jnp.int32),
            pltpu.VMEM((2, TOP_K * tc // 8, 8, D), F32),
            pltpu.SemaphoreType.DMA,
            pltpu.SemaphoreType.DMA((2,)),
        ],
        compiler_params=_cparams(1),
    )(pos_tiles, ys, x1, tg)


def _route(top_idx, n_experts, blk):
    T = top_idx.shape[0]
    N = T * TOP_K
    e_flat = top_idx.reshape(N)
    tok_flat = jnp.repeat(jnp.arange(T, dtype=jnp.int32), TOP_K)
    onehot = (e_flat[:, None] == jnp.arange(n_experts, dtype=jnp.int32)[None, :]).astype(jnp.int32)
    csum = jnp.cumsum(onehot, axis=0)
    rank = jnp.sum(csum * onehot, axis=1) - 1
    counts = csum[-1]
    padded = ((counts + blk - 1) // blk) * blk
    pend = jnp.cumsum(padded)
    pstart = pend - padded
    dest = pstart[e_flat] + rank
    n_blocks = -(-(N + n_experts * (blk - 1)) // blk)
    rows = n_blocks * blk
    row_tok = jnp.zeros((rows,), jnp.int32).at[dest].set(tok_flat)
    starts = jnp.arange(n_blocks, dtype=jnp.int32) * blk
    block_e = jnp.minimum(jnp.sum((pend[None, :] <= starts[:, None]).astype(jnp.int32), axis=1),
                          n_experts - 1).astype(jnp.int32)
    n_used = (pend[-1:] // blk).astype(jnp.int32)
    return dest, row_tok.reshape(n_blocks, blk), block_e, n_used


def _cmp_to_sel(nc, n_sel):
    ratio = SEL_BLOCK // CMP_STRIDE
    offs = (np.arange(ratio)[:, None] - np.arange(CMP_LEN // CMP_STRIDE)[None, :]).reshape(-1)
    diff = np.arange(nc)[:, None] - ratio * np.arange(n_sel)[None, :]
    return (diff[..., None] == offs).sum(-1).astype(np.float32)


def _pick(n, pref):
    for c in pref:
        if n % c == 0:
            return c
    return n


def kernel(x, mem, positions, mix_norm_g, mem_norm_g, w_in, w_out, gmlp_ws, gmlp_bs, gmlp_vg,
           nsa_cmp_pos, nsa_ck1, nsa_ck2, nsa_cv1, nsa_cv2, nsa_q_g, nsa_k_g,
           xattn_wkv, xattn_q_g, xattn_k_g, ffn_norm_g, w_router, b_router,
           w_e1, b_e1, w_e2, b_e2):
    B, S, D = x.shape
    T = B * S
    depth = w_in.shape[0]
    n_experts = w_router.shape[-1]
    NC = S // CMP_STRIDE
    n_sel = S // SEL_BLOCK
    tm = _pick(S, (512, 256, 128))
    tk = _pick(S, (512, 256, 128))

    half = HEAD_DIM // 2
    inv = ROPE_THETA ** (-jnp.arange(half, dtype=F32) * 2.0 / HEAD_DIM)
    ang = (positions.astype(F32).reshape(T, 1) * jnp.concatenate([inv, inv])[None, :])
    c2st = jnp.asarray(_cmp_to_sel(NC, n_sel).T, BF16)

    xs = x.reshape(T, D)
    for l in range(depth):
        wi = w_in[l]
        n_gate = 3 * NSA_REP
        zpad = jnp.zeros((D, 128 - n_gate), wi.dtype)
        w_r = jnp.concatenate(
            [wi[:, :3584], wi[:, 3608:4120],
             wi[:, 3584:3584 + n_gate], zpad, wi[:, 3584 + n_gate:3608], zpad], axis=1).astype(BF16)

        proj = _in_proj(xs, mix_norm_g[l][None, :], w_r, tm, PROJ_COLS // 2)

        bias_full = jnp.repeat(jnp.transpose(gmlp_bs[l]), HEAD_DIM, axis=1)
        y_a = _gmlp(proj, gmlp_ws[l], bias_full, gmlp_vg[l].reshape(1, GMLP_WIDTH), T, tm)

        qn, qr, ks, vs, kw, vw = _prep(proj, ang, nsa_q_g[l][None, :], nsa_k_g[l][None, :], B, S, tm)

        w1 = jnp.stack([nsa_ck1[l], nsa_cv1[l]]).astype(BF16)
        w2 = jnp.stack([nsa_ck2[l], nsa_cv2[l]]).astype(BF16)
        cmp_kv = _compress(proj.reshape(B, NC, CMP_STRIDE, PROJ_COLS), w1, w2,
                           nsa_cmp_pos[l], nsa_k_g[l][None, :], B, NC)

        o_c, sel = _cmp_attn(qn, cmp_kv, proj, c2st, B, S, _pick(S, (256, 128)))
        o_s = _sel_attn(qr, ks, vs, proj, sel, B, S, _pick(S, (256, 128)), _pick(S, (1024, 512, 256)),
                        _pick(S, (512, 256)))
        o_w = _win_attn(qr, kw, vw, proj, B, S, _pick(S, (256, 128)))

        km, vm = _mem_kv(mem, mem_norm_g[l][None, :], xattn_wkv[l].astype(BF16), xattn_k_g[l][None, :])
        y_c = _xattn(proj, km, vm, xattn_q_g[l][None, :], B, S, tm)

        wr = jnp.pad(w_router[l], ((0, 0), (0, 128 - n_experts)))
        wr_hi = wr.astype(BF16)
        wr_lo = (wr - wr_hi.astype(F32)).astype(BF16)
        br = jnp.pad(b_router[l], (0, 128 - n_experts))[None, :]
        x1, h2, ti, tg = _out_proj(xs, y_a, o_c, o_s, o_w, y_c, w_out[l].astype(BF16),
                                   ffn_norm_g[l][None, :], wr_hi, wr_lo, br, n_experts,
                                   _pick(S, (256, 128)))

        top_idx = ti[:, :TOP_K].astype(jnp.int32)
        dest, row_tok2, block_e, n_used = _route(top_idx, n_experts, MOE_BLOCK)

        d_ff = w_e2.shape[2]
        w1g, w1l = _deinterleave(w_e1[l].reshape(n_experts * D, 2 * d_ff), 512, _pick(2 * d_ff, (2048, 1024, 512, 256)))
        w1g = w1g.reshape(n_experts, D, d_ff)
        w1l = w1l.reshape(n_experts, D, d_ff)
        b1g = b_e1[l][:, None, 0::2]
        b1l = b_e1[l][:, None, 1::2]
        ys = _moe_experts(block_e, n_used, row_tok2, h2, w1g, w1l, b1g, b1l,
                          w_e2[l].astype(BF16), b_e2[l][:, None, :], _pick(d_ff, (1024, 512)),
                          _pick(D, (512, 256)))

        tc = 128
        pos_tiles = dest.reshape(T // tc, tc, TOP_K).transpose(0, 2, 1).reshape(T // tc, TOP_K * tc)
        xs = _combine(pos_tiles, ys, x1, tg, tc)
    return xs.reshape(B, S, D)
```

```python
import functools

import numpy as np
import jax
import jax.numpy as jnp
from jax import lax
from jax.experimental import pallas as pl
from jax.experimental.pallas import tpu as pltpu

F32 = jnp.float32
BF16 = jnp.bfloat16

HEAD_DIM = 128
GMLP_GROUPS = 4
NSA_HEADS = 8
NSA_KV_HEADS = 2
NSA_REP = NSA_HEADS // NSA_KV_HEADS
XATTN_HEADS = 4
GMLP_WIDTH = GMLP_GROUPS * HEAD_DIM
NSA_WIDTH = NSA_HEADS * HEAD_DIM
XATTN_WIDTH = XATTN_HEADS * HEAD_DIM
GMLP_CHUNK = 128
CMP_LEN = 32
CMP_STRIDE = 16
SEL_BLOCK = 64
SEL_TOPK = 16
WINDOW = 512
Q_BLOCK = 128
TOP_K = 4
SWIGLU_ALPHA = 1.702
SWIGLU_LIMIT = 7.0
MOE_BLOCK = 512
ROPE_THETA = 10000.0
EPS = 1e-6
NEG_BIG = -1e30
MASK_BIG = 2.0 ** 100

COL_UV = 0
COL_Q = 8
COL_KV = 16
COL_QX = 28
COL_GATE = 32
N_COL_BLOCKS = 34
PROJ_COLS = N_COL_BLOCKS * 128

VMEM_LIMIT = 56 * 1024 * 1024


def _cparams(n_axes):
    return pltpu.CompilerParams(
        dimension_semantics=("arbitrary",) * n_axes, vmem_limit_bytes=VMEM_LIMIT)


def _rms(x, g):
    ms = jnp.mean(x * x, axis=-1, keepdims=True)
    return x * lax.rsqrt(ms + EPS) * g


def _gelu_tanh(x):
    c = np.float32(np.sqrt(2.0 / np.pi))
    return 0.5 * x * (1.0 + jnp.tanh(c * (x + 0.044715 * (x * x * x))))


def _sigmoid(x):
    return 1.0 / (1.0 + jnp.exp(-x))


def _dot(a, b):
    return jnp.dot(a, b, preferred_element_type=F32)


def _dot_nt(a, b):
    return lax.dot_general(a, b, (((1,), (1,)), ((), ())), preferred_element_type=F32)


def _inproj_kernel(x_ref, g_ref, w_ref, o_ref, h_ref):
    @pl.when(pl.program_id(1) == 0)
    def _():
        h_ref[...] = _rms(x_ref[...], g_ref[...]).astype(BF16)

    o_ref[...] = _dot(h_ref[...], w_ref[...])


def _in_proj(x2, g, w_r, tm, tn):
    T, D = x2.shape
    NC = w_r.shape[1]
    return pl.pallas_call(
        _inproj_kernel,
        grid=(T // tm, NC // tn),
        in_specs=[
            pl.BlockSpec((tm, D), lambda i, j: (i, 0)),
            pl.BlockSpec((1, D), lambda i, j: (0, 0)),
            pl.BlockSpec((D, tn), lambda i, j: (0, j)),
        ],
        out_specs=pl.BlockSpec((tm, tn), lambda i, j: (i, j)),
        out_shape=jax.ShapeDtypeStruct((T, NC), F32),
        scratch_shapes=[pltpu.VMEM((tm, D), BF16)],
        compiler_params=_cparams(2),
    )(x2, g, w_r)


def _prep_kernel(q_ref, sel_ref, win_ref, ang_ref, qg_ref, kg_ref,
                 qn_o, qr_o, ks_o, vs_o, kw_o, vw_o):
    ang = ang_ref[...]
    lane = lax.broadcasted_iota(jnp.int32, (1, HEAD_DIM), 1)
    cosf = jnp.cos(ang)
    sinf = jnp.sin(ang) * jnp.where(lane < HEAD_DIM // 2, -1.0, 1.0)
    scale = np.float32(HEAD_DIM ** -0.5 * np.log2(np.e))

    def rope(x):
        return x * cosf + pltpu.roll(x, HEAD_DIM // 2, axis=1) * sinf

    qg = qg_ref[...]
    kg = kg_ref[...]
    for h in range(NSA_HEADS):
        sl = slice(h * HEAD_DIM, (h + 1) * HEAD_DIM)
        n = _rms(q_ref[:, sl], qg)
        qn_o[:, sl] = (n * scale).astype(BF16)
        qr_o[:, sl] = (rope(n) * scale).astype(BF16)
    for g in range(NSA_KV_HEADS):
        sl = slice(g * HEAD_DIM, (g + 1) * HEAD_DIM)
        sv = slice((NSA_KV_HEADS + g) * HEAD_DIM, (NSA_KV_HEADS + g + 1) * HEAD_DIM)
        ks_o[g] = rope(_rms(sel_ref[:, sl], kg)).astype(BF16)
        vs_o[g] = sel_ref[:, sv].astype(BF16)
        kw_o[g] = rope(_rms(win_ref[:, sl], kg)).astype(BF16)
        vw_o[g] = win_ref[:, sv].astype(BF16)


def _prep(proj, ang, q_g, k_g, B, S, tm):
    T = B * S
    nsb = S // tm
    kv_shape = jax.ShapeDtypeStruct((B, NSA_KV_HEADS, S, HEAD_DIM), BF16)
    kv_spec = pl.BlockSpec((None, NSA_KV_HEADS, tm, HEAD_DIM),
                           lambda i: (i // nsb, 0, i % nsb, 0))
    return pl.pallas_call(
        _prep_kernel,
        grid=(T // tm,),
        in_specs=[
            pl.BlockSpec((tm, NSA_WIDTH), lambda i: (i, COL_Q * 128 // NSA_WIDTH)),
            pl.BlockSpec((tm, 512), lambda i: (i, (COL_KV + 4) * 128 // 512)),
            pl.BlockSpec((tm, 512), lambda i: (i, (COL_KV + 8) * 128 // 512)),
            pl.BlockSpec((tm, HEAD_DIM), lambda i: (i, 0)),
            pl.BlockSpec((1, HEAD_DIM), lambda i: (0, 0)),
            pl.BlockSpec((1, HEAD_DIM), lambda i: (0, 0)),
        ],
        out_specs=[
            pl.BlockSpec((tm, NSA_WIDTH), lambda i: (i, 0)),
            pl.BlockSpec((tm, NSA_WIDTH), lambda i: (i, 0)),
            kv_spec, kv_spec, kv_spec, kv_spec,
        ],
        out_shape=[
            jax.ShapeDtypeStruct((T, NSA_WIDTH), BF16),
            jax.ShapeDtypeStruct((T, NSA_WIDTH), BF16),
            kv_shape, kv_shape, kv_shape, kv_shape,
        ],
        compiler_params=_cparams(1),
    )(proj, proj, proj, ang, q_g, k_g)


def _gmlp_kernel(uv_ref, ws_ref, bias_ref, gv_ref, o_ref, *, n_chunks):
    row = lax.broadcasted_iota(jnp.int32, (GMLP_CHUNK, GMLP_CHUNK), 0)
    col = lax.broadcasted_iota(jnp.int32, (GMLP_CHUNK, GMLP_CHUNK), 1)
    causal = col <= row
    for g in range(GMLP_GROUPS):
        sl = slice(g * HEAD_DIM, (g + 1) * HEAD_DIM)
        sv = slice(GMLP_WIDTH + g * HEAD_DIM, GMLP_WIDTH + (g + 1) * HEAD_DIM)
        w = jnp.where(causal, ws_ref[g], 0.0).astype(BF16)
        u = _gelu_tanh(uv_ref[:, sl])
        v = _rms(_gelu_tanh(uv_ref[:, sv]), gv_ref[:, sl]).astype(BF16)
        bias = bias_ref[:, sl]
        for c in range(n_chunks):
            rs = slice(c * GMLP_CHUNK, (c + 1) * GMLP_CHUNK)
            vs = _dot(w, v[rs]) + bias
            o_ref[rs, sl] = (u[rs] * vs).astype(BF16)


def _gmlp(proj, ws, bias_full, gv, T, tm):
    return pl.pallas_call(
        functools.partial(_gmlp_kernel, n_chunks=tm // GMLP_CHUNK),
        grid=(T // tm,),
        in_specs=[
            pl.BlockSpec((tm, 2 * GMLP_WIDTH), lambda i: (i, 0)),
            pl.BlockSpec((GMLP_GROUPS, GMLP_CHUNK, GMLP_CHUNK), lambda i: (0, 0, 0)),
            pl.BlockSpec((GMLP_CHUNK, GMLP_WIDTH), lambda i: (0, 0)),
            pl.BlockSpec((1, GMLP_WIDTH), lambda i: (0, 0)),
        ],
        out_specs=pl.BlockSpec((tm, GMLP_WIDTH), lambda i: (i, 0)),
        out_shape=jax.ShapeDtypeStruct((T, GMLP_WIDTH), BF16),
        compiler_params=_cparams(1),
    )(proj, ws, bias_full, gv)


def _compress_kernel(x_ref, w1_ref, w2_ref, pos_ref, kg_ref, o_ref):
    nc = x_ref.shape[0]
    kind = pl.program_id(1)
    acc_a = jnp.zeros((nc, HEAD_DIM), F32)
    acc_b = jnp.zeros((nc, HEAD_DIM), F32)
    for j in range(CMP_STRIDE):
        xj = x_ref[:, j, :]
        wa = w1_ref[j * HEAD_DIM:(j + 1) * HEAD_DIM, :]
        wb = w1_ref[(CMP_STRIDE + j) * HEAD_DIM:(CMP_STRIDE + j + 1) * HEAD_DIM, :]
        acc_a += _dot((xj + pos_ref[j:j + 1, :]).astype(BF16), wa)
        acc_b += _dot((xj + pos_ref[CMP_STRIDE + j:CMP_STRIDE + j + 1, :]).astype(BF16), wb)
    hidden = acc_a + pltpu.roll(acc_b, nc - 1, axis=0)
    out = _dot(_gelu_tanh(hidden).astype(BF16), w2_ref[...])
    o_ref[...] = jnp.where(kind == 0, _rms(out, kg_ref[...]), out).astype(BF16)


def _compress(proj4, w1, w2, cmp_pos, k_g, B, NC):
    return pl.pallas_call(
        _compress_kernel,
        grid=(B, 2, NSA_KV_HEADS),
        in_specs=[
            pl.BlockSpec((None, NC, CMP_STRIDE, HEAD_DIM),
                         lambda b, k, g: (b, 0, 0, COL_KV + 2 * k + g)),
            pl.BlockSpec((None, CMP_LEN * HEAD_DIM, HEAD_DIM), lambda b, k, g: (k, 0, 0)),
            pl.BlockSpec((None, HEAD_DIM, HEAD_DIM), lambda b, k, g: (k, 0, 0)),
            pl.BlockSpec((CMP_LEN, HEAD_DIM), lambda b, k, g: (0, 0)),
            pl.BlockSpec((1, HEAD_DIM), lambda b, k, g: (0, 0)),
        ],
        out_specs=pl.BlockSpec((None, None, None, NC, HEAD_DIM), lambda b, k, g: (b, k, g, 0, 0)),
        out_shape=jax.ShapeDtypeStruct((B, 2, NSA_KV_HEADS, NC, HEAD_DIM), BF16),
        compiler_params=_cparams(3),
    )(proj4, w1, w2, cmp_pos, k_g)


def _cmp_attn_kernel(q_ref, kc_ref, vc_ref, gl_ref, c2st_ref, oc_ref, sel_ref, q_sc, imp_sc,
                     *, k_top, widths):
    i = pl.program_id(2)
    tq = q_ref.shape[0]
    n_sel = c2st_ref.shape[0]
    t = i * tq + lax.broadcasted_iota(jnp.int32, (tq, 1), 0)
    gl = gl_ref[...]
    for r in range(NSA_REP):
        q_sc[r * tq:(r + 1) * tq, :] = q_ref[:, r * HEAD_DIM:(r + 1) * HEAD_DIM]

    def attend(nc):
        cmp_end = lax.broadcasted_iota(jnp.int32, (1, nc), 1) * CMP_STRIDE + (CMP_LEN - 1)
        bias = jnp.where(cmp_end <= t, 0.0, -jnp.inf)
        s = _dot_nt(q_sc[...], kc_ref[0:nc, :]) + jnp.concatenate([bias] * NSA_REP, axis=0)
        m = jnp.max(s, axis=-1, keepdims=True)
        m = jnp.where(m > -jnp.inf, m, 0.0)
        e = jnp.exp2(s - m)
        d = jnp.sum(e, axis=-1, keepdims=True)
        p = e * (1.0 / jnp.where(d > 0, d, 1.0))
        o = _dot(p.astype(BF16), vc_ref[0:nc, :])
        pc_sum = jnp.zeros((tq, nc), F32)
        for r in range(NSA_REP):
            rs = slice(r * tq, (r + 1) * tq)
            gate = _sigmoid(gl[:, 3 * r:3 * r + 1])
            oc_ref[:, r * HEAD_DIM:(r + 1) * HEAD_DIM] = o[rs, :] * gate
            pc_sum += p[rs, :]
        hi = pc_sum.astype(BF16)
        lo = (pc_sum - hi.astype(F32)).astype(BF16)
        imp_sc[...] = _dot_nt(c2st_ref[:, 0:nc], hi) + _dot_nt(c2st_ref[:, 0:nc], lo)

    needed = (i * tq + tq - CMP_LEN) // CMP_STRIDE + 1
    for j, w in enumerate(widths):
        conds = ([needed <= w] if j + 1 < len(widths) else []) + ([needed > widths[j - 1]] if j else [])
        if conds:
            pl.when(functools.reduce(jnp.logical_and, conds))(functools.partial(attend, w))
        else:
            attend(w)

    imp = imp_sc[...]
    sid = lax.broadcasted_iota(jnp.int32, (n_sel, 1), 0)
    sid_f = sid.astype(F32)
    tt = i * tq + lax.broadcasted_iota(jnp.int32, (1, tq), 1)
    cur = tt // SEL_BLOCK
    valid = sid * SEL_BLOCK <= tt
    forced = (sid == 0) | (sid == cur) | (sid == cur - 1)
    score = jnp.where(valid, jnp.where(forced, jnp.inf, imp), -jnp.inf)
    sel = jnp.zeros((n_sel, tq), F32)
    for _ in range(k_top):
        m = jnp.max(score, axis=0, keepdims=True)
        idx = jnp.min(jnp.where(score == m, sid_f, np.float32(n_sel)), axis=0, keepdims=True)
        pick = sid_f == idx
        sel = jnp.where(pick, 1.0, sel)
        score = jnp.where(pick, -jnp.inf, score)
    selm = jnp.transpose(sel - 1.0)
    pad = sel_ref.shape[1] - n_sel
    if pad:
        selm = jnp.concatenate([selm, jnp.full((tq, pad), -1.0, F32)], axis=1)
    sel_ref[...] = selm


def _cmp_attn(qn, cmp_kv, proj, c2st, B, S, tq):
    T = B * S
    nqb = S // tq
    NC = cmp_kv.shape[3]
    n_sel = S // SEL_BLOCK
    n_sel_pad = -(-n_sel // 128) * 128
    gw = NSA_REP * HEAD_DIM
    return pl.pallas_call(
        functools.partial(_cmp_attn_kernel, k_top=min(SEL_TOPK, n_sel),
                          widths=tuple(NC * (j + 1) // 4 for j in range(4)) if NC % 1024 == 0 else (NC,)),
        grid=(B, NSA_KV_HEADS, nqb),
        in_specs=[
            pl.BlockSpec((tq, gw), lambda b, g, i: (b * nqb + i, g)),
            pl.BlockSpec((None, None, None, NC, HEAD_DIM), lambda b, g, i: (b, 0, g, 0, 0)),
            pl.BlockSpec((None, None, None, NC, HEAD_DIM), lambda b, g, i: (b, 1, g, 0, 0)),
            pl.BlockSpec((tq, 128), lambda b, g, i: (b * nqb + i, COL_GATE + g)),
            pl.BlockSpec((n_sel, NC), lambda b, g, i: (0, 0)),
        ],
        out_specs=[
            pl.BlockSpec((tq, gw), lambda b, g, i: (b * nqb + i, g)),
            pl.BlockSpec((None, None, tq, n_sel_pad), lambda b, g, i: (b, g, i, 0)),
        ],
        out_shape=[
            jax.ShapeDtypeStruct((T, NSA_WIDTH), F32),
            jax.ShapeDtypeStruct((B, NSA_KV_HEADS, S, n_sel_pad), F32),
        ],
        scratch_shapes=[pltpu.VMEM((NSA_REP * tq, HEAD_DIM), BF16), pltpu.VMEM((n_sel, tq), F32)],
        compiler_params=_cparams(3),
    )(qn, cmp_kv, cmp_kv, proj, c2st)


def _sel_attn_kernel(q_ref, k_ref, v_ref, gl_ref, sel_ref, e0_ref, o_ref,
                     q_sc, m_sc, l_sc, acc_sc, *, tk, tkd):
    i = pl.program_id(2)
    tq = q_ref.shape[0]
    s0 = i * tq
    t = s0 + lax.broadcasted_iota(jnp.int32, (tq, 1), 0)
    for r in range(NSA_REP):
        q_sc[r * tq:(r + 1) * tq, :] = q_ref[:, r * HEAD_DIM:(r + 1) * HEAD_DIM]
    m_sc[...] = jnp.full(m_sc.shape, NEG_BIG, F32)
    l_sc[...] = jnp.zeros(l_sc.shape, F32)
    acc_sc[...] = jnp.zeros(acc_sc.shape, F32)
    def step(koff, width, diagonal):
        koff = pl.multiple_of(koff, width)
        k = k_ref[pl.ds(koff, width), :]
        v = v_ref[pl.ds(koff, width), :]
        c0 = koff // SEL_BLOCK
        slab = sel_ref[:, pl.ds(pl.multiple_of((c0 // 128) * 128, 128), 128)]
        rolled = pltpu.roll(slab, (128 - c0 % 128) % 128, axis=1)
        bias = _dot(rolled.astype(BF16), e0_ref[:, :width])
        if diagonal:
            key = koff + lax.broadcasted_iota(jnp.int32, (1, width), 1)
            bias = jnp.where(key <= t, bias, -MASK_BIG)
        s = _dot_nt(q_sc[...], k) + jnp.concatenate([bias] * NSA_REP, axis=0)
        m_old = m_sc[...]
        m_new = jnp.maximum(m_old, jnp.max(s, axis=-1, keepdims=True))
        alpha = jnp.exp2(m_old - m_new)
        p = jnp.exp2(s - jnp.concatenate([m_new] * (width // 128), axis=1))
        l_sc[...] = alpha * l_sc[...] + jnp.sum(p, axis=-1, keepdims=True)
        acc_sc[...] = alpha * acc_sc[...] + _dot(p.astype(BF16), v)
        m_sc[...] = m_new

    def body(j, carry):
        step(j * tk, tk, False)
        return carry

    n_wide = s0 // tk
    lax.fori_loop(0, n_wide, body, 0)
    tail = n_wide * tk
    n_narrow = (s0 - tail) // tkd
    for u in range(tk // tkd - 1):
        @pl.when(u < n_narrow)
        def _():
            step(tail + u * tkd, tkd, False)

    step(tail + n_narrow * tkd, tkd, True)
    gl = gl_ref[...]
    for r in range(NSA_REP):
        rs = slice(r * tq, (r + 1) * tq)
        gate = _sigmoid(gl[:, 3 * r + 1:3 * r + 2])
        o_ref[:, r * HEAD_DIM:(r + 1) * HEAD_DIM] = acc_sc[rs, :] / l_sc[rs, :] * gate


def _sel_attn(qr, ks, vs, proj, sel, B, S, tq, tk, tkd):
    assert tk % tkd == 0 and tkd % tq == 0 and tkd % SEL_BLOCK == 0 and tk // SEL_BLOCK <= 128
    T = B * S
    nqb = S // tq
    n_sel = S // SEL_BLOCK
    gw = NSA_REP * HEAD_DIM
    n_sel_pad = sel.shape[-1]
    e0 = np.where(np.arange(128)[:, None] == np.arange(tk)[None, :] // SEL_BLOCK, MASK_BIG, 0.0)
    return pl.pallas_call(
        functools.partial(_sel_attn_kernel, tk=tk, tkd=tkd),
        grid=(B, NSA_KV_HEADS, nqb),
        in_specs=[
            pl.BlockSpec((tq, gw), lambda b, g, i: (b * nqb + i, g)),
            pl.BlockSpec((None, None, S, HEAD_DIM), lambda b, g, i: (b, g, 0, 0)),
            pl.BlockSpec((None, None, S, HEAD_DIM), lambda b, g, i: (b, g, 0, 0)),
            pl.BlockSpec((tq, 128), lambda b, g, i: (b * nqb + i, COL_GATE + g)),
            pl.BlockSpec((None, None, tq, n_sel_pad), lambda b, g, i: (b, g, i, 0)),
            pl.BlockSpec((128, tk), lambda b, g, i: (0, 0)),
        ],
        out_specs=pl.BlockSpec((tq, gw), lambda b, g, i: (b * nqb + i, g)),
        out_shape=jax.ShapeDtypeStruct((T, NSA_WIDTH), F32),
        scratch_shapes=[
            pltpu.VMEM((NSA_REP * tq, HEAD_DIM), BF16),
            pltpu.VMEM((NSA_REP * tq, 128), F32),
            pltpu.VMEM((NSA_REP * tq, 128), F32),
            pltpu.VMEM((NSA_REP * tq, HEAD_DIM), F32),
        ],
        compiler_params=_cparams(3),
    )(qr, ks, vs, proj, sel, jnp.asarray(e0, BF16))


def _win_attn_kernel(q_ref, k_ref, v_ref, gl_ref, o_ref, q_sc, *, wk):
    i = pl.program_id(2)
    tq = q_ref.shape[0]
    s0 = i * tq
    t = s0 + lax.broadcasted_iota(jnp.int32, (tq, 1), 0)
    for r in range(NSA_REP):
        q_sc[r * tq:(r + 1) * tq, :] = q_ref[:, r * HEAD_DIM:(r + 1) * HEAD_DIM]
    ws = pl.multiple_of(jnp.maximum(s0 - WINDOW, 0), tq)
    k = k_ref[pl.ds(ws, wk), :]
    v = v_ref[pl.ds(ws, wk), :]
    kp = ws + lax.broadcasted_iota(jnp.int32, (1, wk), 1)
    bias = jnp.where(kp <= t, jnp.where(kp > t - WINDOW, 0.0, NEG_BIG), NEG_BIG)
    s = _dot_nt(q_sc[...], k) + jnp.concatenate([bias] * NSA_REP, axis=0)
    m = jnp.max(s, axis=-1, keepdims=True)
    e = jnp.exp2(s - m)
    p = e * (1.0 / jnp.sum(e, axis=-1, keepdims=True))
    o = _dot(p.astype(BF16), v)
    gl = gl_ref[...]
    for r in range(NSA_REP):
        gate = _sigmoid(gl[:, 3 * r + 2:3 * r + 3])
        o_ref[:, r * HEAD_DIM:(r + 1) * HEAD_DIM] = o[r * tq:(r + 1) * tq, :] * gate


def _win_attn(qr, kw, vw, proj, B, S, tq):
    T = B * S
    nqb = S // tq
    gw = NSA_REP * HEAD_DIM
    wk = min(WINDOW + tq, S)
    return pl.pallas_call(
        functools.partial(_win_attn_kernel, wk=wk),
        grid=(B, NSA_KV_HEADS, nqb),
        in_specs=[
            pl.BlockSpec((tq, gw), lambda b, g, i: (b * nqb + i, g)),
            pl.BlockSpec((None, None, S, HEAD_DIM), lambda b, g, i: (b, g, 0, 0)),
            pl.BlockSpec((None, None, S, HEAD_DIM), lambda b, g, i: (b, g, 0, 0)),
            pl.BlockSpec((tq, 128), lambda b, g, i: (b * nqb + i, COL_GATE + g)),
        ],
        out_specs=pl.BlockSpec((tq, gw), lambda b, g, i: (b * nqb + i, g)),
        out_shape=jax.ShapeDtypeStruct((T, NSA_WIDTH), F32),
        scratch_shapes=[pltpu.VMEM((NSA_REP * tq, HEAD_DIM), BF16)],
        compiler_params=_cparams(3),
    )(qr, kw, vw, proj)


def _memkv_kernel(mem_ref, g_ref, w_ref, kg_ref, k_o, v_o):
    h = _rms(mem_ref[...], g_ref[...]).astype(BF16)
    kv = _dot(h, w_ref[...])
    kg = kg_ref[...]
    for hd in range(XATTN_HEADS):
        sl = slice(hd * HEAD_DIM, (hd + 1) * HEAD_DIM)
        sv = slice(XATTN_WIDTH + hd * HEAD_DIM, XATTN_WIDTH + (hd + 1) * HEAD_DIM)
        k_o[:, sl] = _rms(kv[:, sl], kg).astype(BF16)
        v_o[:, sl] = kv[:, sv].astype(BF16)


def _mem_kv(mem, g, wkv, k_g):
    B, M, D = mem.shape
    shp = jax.ShapeDtypeStruct((B, M, XATTN_WIDTH), BF16)
    spec = pl.BlockSpec((None, M, XATTN_WIDTH), lambda b: (b, 0, 0))
    return pl.pallas_call(
        _memkv_kernel,
        grid=(B,),
        in_specs=[
            pl.BlockSpec((None, M, D), lambda b: (b, 0, 0)),
            pl.BlockSpec((1, D), lambda b: (0, 0)),
            pl.BlockSpec((D, 2 * XATTN_WIDTH), lambda b: (0, 0)),
            pl.BlockSpec((1, HEAD_DIM), lambda b: (0, 0)),
        ],
        out_specs=[spec, spec],
        out_shape=[shp, shp],
        compiler_params=_cparams(1),
    )(mem, g, wkv, k_g)


def _xattn_kernel(q_ref, k_ref, v_ref, qg_ref, o_ref):
    scale = np.float32(HEAD_DIM ** -0.5)
    qg = qg_ref[...]
    for hd in range(XATTN_HEADS):
        sl = slice(hd * HEAD_DIM, (hd + 1) * HEAD_DIM)
        q = (_rms(q_ref[:, sl], qg) * scale).astype(BF16)
        s = _dot_nt(q, k_ref[:, sl])
        m = jnp.max(s, axis=-1, keepdims=True)
        e = jnp.exp(s - m)
        p = e * (1.0 / jnp.sum(e, axis=-1, keepdims=True))
        o_ref[:, sl] = _dot(p.astype(BF16), v_ref[:, sl]).astype(BF16)


def _xattn(proj, km, vm, q_g, B, S, tm):
    T = B * S
    nsb = S // tm
    M = km.shape[1]
    return pl.pallas_call(
        _xattn_kernel,
        grid=(T // tm,),
        in_specs=[
            pl.BlockSpec((tm, XATTN_WIDTH), lambda i: (i, COL_QX * 128 // XATTN_WIDTH)),
            pl.BlockSpec((None, M, XATTN_WIDTH), lambda i: (i // nsb, 0, 0)),
            pl.BlockSpec((None, M, XATTN_WIDTH), lambda i: (i // nsb, 0, 0)),
            pl.BlockSpec((1, HEAD_DIM), lambda i: (0, 0)),
        ],
        out_specs=pl.BlockSpec((tm, XATTN_WIDTH), lambda i: (i, 0)),
        out_shape=jax.ShapeDtypeStruct((T, XATTN_WIDTH), BF16),
        compiler_params=_cparams(1),
    )(proj, km, vm, q_g)


def _outproj_kernel(x_ref, ya_ref, oc_ref, os_ref, ow_ref, yc_ref, wo_ref, g_ref,
                    wrh_ref, wrl_ref, br_ref, x1_o, h2_o, ti_o, tg_o, *, n_experts):
    yb = (oc_ref[...] + os_ref[...] + ow_ref[...]).astype(BF16)
    x1 = x_ref[...]
    x1 += _dot(ya_ref[...], wo_ref[0:GMLP_WIDTH, :])
    x1 += _dot(yb, wo_ref[GMLP_WIDTH:GMLP_WIDTH + NSA_WIDTH, :])
    x1 += _dot(yc_ref[...], wo_ref[GMLP_WIDTH + NSA_WIDTH:, :])
    x1_o[...] = x1
    h2 = _rms(x1, g_ref[...])
    h2_o[...] = h2
    hi = h2.astype(BF16)
    lo = (h2 - hi.astype(F32)).astype(BF16)
    logits = (_dot(hi, wrh_ref[...]) + _dot(lo, wrh_ref[...]) + _dot(hi, wrl_ref[...])
              + br_ref[...])
    lane = lax.broadcasted_iota(jnp.int32, (1, 128), 1)
    lane_f = lane.astype(F32)
    lg = jnp.where(lane < n_experts, logits, -jnp.inf)
    ti = jnp.zeros(logits.shape, F32)
    tg = jnp.zeros(logits.shape, F32)
    denom = jnp.zeros((logits.shape[0], 1), F32)
    v0 = None
    for k in range(TOP_K):
        m = jnp.max(lg, axis=-1, keepdims=True)
        ix = jnp.min(jnp.where(lg == m, lane_f, 128.0), axis=-1, keepdims=True)
        if v0 is None:
            v0 = m
        e = jnp.exp(m - v0)
        denom += e
        ti = jnp.where(lane == k, ix, ti)
        tg = jnp.where(lane == k, e, tg)
        lg = jnp.where(lane_f == ix, -jnp.inf, lg)
    ti_o[...] = ti
    tg_o[...] = tg / denom


def _out_proj(x2, ya, oc, os_, ow, yc, wo, g, wr_hi, wr_lo, br, n_experts, tm):
    T, D = x2.shape
    row = lambda w: pl.BlockSpec((tm, w), lambda i: (i, 0))
    full = lambda a: pl.BlockSpec(a.shape, lambda i: (0,) * a.ndim)
    return pl.pallas_call(
        functools.partial(_outproj_kernel, n_experts=n_experts),
        grid=(T // tm,),
        in_specs=[row(D), row(GMLP_WIDTH), row(NSA_WIDTH), row(NSA_WIDTH), row(NSA_WIDTH),
                  row(XATTN_WIDTH), full(wo), full(g), full(wr_hi), full(wr_lo), full(br)],
        out_specs=[row(D), row(D), row(128), row(128)],
        out_shape=[
            jax.ShapeDtypeStruct((T, D), F32),
            jax.ShapeDtypeStruct((T, D), F32),
            jax.ShapeDtypeStruct((T, 128), F32),
            jax.ShapeDtypeStruct((T, 128), F32),
        ],
        compiler_params=_cparams(1),
    )(x2, ya, oc, os_, ow, yc, wo, g, wr_hi, wr_lo, br)


def _deinterleave_kernel(w_ref, p_ref, g_o, l_o):
    perm = p_ref[...]
    for c in range(w_ref.shape[1] // 256):
        t = _dot(w_ref[:, c * 256:(c + 1) * 256].astype(BF16), perm)
        g_o[:, c * 128:(c + 1) * 128] = t[:, :128].astype(BF16)
        l_o[:, c * 128:(c + 1) * 128] = t[:, 128:].astype(BF16)


def _deinterleave(w, tr, tcw):
    R, C = w.shape
    src = np.arange(256)
    dst = np.where(src % 2 == 0, src // 2, 128 + src // 2)
    perm = np.zeros((256, 256), np.float32)
    perm[src, dst] = 1.0
    out = jax.ShapeDtypeStruct((R, C // 2), BF16)
    return pl.pallas_call(
        _deinterleave_kernel,
        grid=(R // tr, C // tcw),
        in_specs=[
            pl.BlockSpec((tr, tcw), lambda i, j: (i, j)),
            pl.BlockSpec((256, 256), lambda i, j: (0, 0)),
        ],
        out_specs=[pl.BlockSpec((tr, tcw // 2), lambda i, j: (i, j))] * 2,
        out_shape=[out, out],
        compiler_params=_cparams(2),
    )(w, jnp.asarray(perm, BF16))


def _moe_kernel(be_ref, nu_ref, rt_hbm, h2_hbm, w1g_ref, w1l_ref, b1g_ref, b1l_ref,
                w2_ref, b2_ref, y_ref, idx_smem, xf_ref, xb_ref, sem_i, sem_g, *, tn):
    b = pl.program_id(0)
    f = pl.program_id(1)
    blk = xb_ref.shape[0]
    n_used = nu_ref[0]
    used = b < n_used
    slot = b % 2

    def row_copy(tok, g, u, sl):
        return pltpu.make_async_copy(h2_hbm.at[pl.ds(tok, 1), :], xf_ref.at[sl, g, pl.ds(u, 1), :],
                                     sem_g.at[sl])

    def start_gather(blk_id, sl):
        cp = pltpu.make_async_copy(rt_hbm.at[blk_id], idx_smem.at[sl], sem_i)
        cp.start()
        cp.wait()

        def issue(g, c):
            for u in range(8):
                row_copy(idx_smem[sl, g * 8 + u], g, u, sl).start()
            return c

        lax.fori_loop(0, blk // 8, issue, 0)

    @pl.when(used & (f == 0))
    def _():
        @pl.when(b == 0)
        def _():
            start_gather(0, 0)

        def drain(g, c):
            for u in range(8):
                row_copy(0, g, u, slot).wait()
            return c

        lax.fori_loop(0, blk // 8, drain, 0)
        xb_ref[...] = xf_ref[slot].reshape(blk, xb_ref.shape[1]).astype(BF16)

        @pl.when(b + 1 < n_used)
        def _():
            start_gather(b + 1, 1 - slot)

        y_ref[...] = jnp.broadcast_to(b2_ref[...], y_ref.shape)

    @pl.when(used)
    def _():
        x = xb_ref[...]
        glu = jnp.minimum(_dot(x, w1g_ref[...]) + b1g_ref[...], SWIGLU_LIMIT)
        lin = jnp.clip(_dot(x, w1l_ref[...]) + b1l_ref[...], -SWIGLU_LIMIT, SWIGLU_LIMIT)
        act = (glu * _sigmoid(SWIGLU_ALPHA * glu) * (lin + 1.0)).astype(BF16)
        for j in range(y_ref.shape[1] // tn):
            cs = slice(j * tn, (j + 1) * tn)
            y_ref[:, cs] += _dot(act, w2_ref[:, cs])

    @pl.when(jnp.logical_not(used) & (f == 0))
    def _():
        y_ref[...] = jnp.zeros(y_ref.shape, F32)


def _moe_experts(block_e, n_used, row_tok2, h2, w1g, w1l, b1g, b1l, w2, b2, tf, tn):
    n_blocks, blk = row_tok2.shape
    E, D, F = w1g.shape
    nf = F // tf

    def bb(b, nu):
        return jnp.minimum(b, nu[0] - 1)

    def ff(b, f, nu):
        return jnp.where(b < nu[0], f, nf - 1)

    grid_spec = pltpu.PrefetchScalarGridSpec(
        num_scalar_prefetch=2,
        grid=(n_blocks, nf),
        in_specs=[
            pl.BlockSpec(memory_space=pl.ANY),
            pl.BlockSpec(memory_space=pl.ANY),
            pl.BlockSpec((None, D, tf), lambda b, f, be, nu: (be[bb(b, nu)], 0, ff(b, f, nu))),
            pl.BlockSpec((None, D, tf), lambda b, f, be, nu: (be[bb(b, nu)], 0, ff(b, f, nu))),
            pl.BlockSpec((None, 1, tf), lambda b, f, be, nu: (be[bb(b, nu)], 0, ff(b, f, nu))),
            pl.BlockSpec((None, 1, tf), lambda b, f, be, nu: (be[bb(b, nu)], 0, ff(b, f, nu))),
            pl.BlockSpec((None, tf, D), lambda b, f, be, nu: (be[bb(b, nu)], ff(b, f, nu), 0)),
            pl.BlockSpec((None, 1, D), lambda b, f, be, nu: (be[bb(b, nu)], 0, 0)),
        ],
        out_specs=pl.BlockSpec((blk, D), lambda b, f, be, nu: (b, 0)),
        scratch_shapes=[
            pltpu.SMEM((2, blk), jnp.int32),
            pltpu.VMEM((2, blk // 8, 8, D), F32),
            pltpu.VMEM((blk, D), BF16),
            pltpu.SemaphoreType.DMA,
            pltpu.SemaphoreType.DMA((2,)),
        ],
    )
    return pl.pallas_call(
        functools.partial(_moe_kernel, tn=tn),
        grid_spec=grid_spec,
        out_shape=jax.ShapeDtypeStruct((n_blocks * blk, D), F32),
        compiler_params=_cparams(2),
    )(block_e, n_used, row_tok2, h2, w1g, w1l, b1g, b1l, w2, b2)


def _combine_kernel(pos_hbm, ys_hbm, x1_ref, tg_ref, o_ref, idx_smem, buf_ref, sem_i, sem_g):
    i = pl.program_id(0)
    tc = x1_ref.shape[0]
    n = idx_smem.shape[1]
    slot = i % 2

    def row_copy(src, g, u, sl):
        return pltpu.make_async_copy(ys_hbm.at[pl.ds(src, 1), :], buf_ref.at[sl, g, pl.ds(u, 1), :],
                                     sem_g.at[sl])

    def start_gather(step, sl):
        cp = pltpu.make_async_copy(pos_hbm.at[step], idx_smem.at[sl], sem_i)
        cp.start()
        cp.wait()

        def issue(g, c):
            for u in range(8):
                row_copy(idx_smem[sl, g * 8 + u], g, u, sl).start(priority=u % 2)
            return c

        lax.fori_loop(0, n // 8, issue, 0)

    @pl.when(i == 0)
    def _():
        start_gather(0, 0)

    @pl.when(i + 1 < pl.num_programs(0))
    def _():
        start_gather(i + 1, 1 - slot)

    def drain(g, c):
        for u in range(8):
            row_copy(0, g, u, slot).wait()
        return c

    lax.fori_loop(0, n // 8, drain, 0)
    acc = x1_ref[...]
    tg = tg_ref[...]
    gk = tc // 8
    for k in range(TOP_K):
        rows = buf_ref[slot, k * gk:(k + 1) * gk].reshape(tc, acc.shape[1])
        acc += rows * tg[:, k:k + 1]
    o_ref[...] = acc


def _combine(pos_tiles, ys, x1, tg, tc):
    T, D = x1.shape
    return pl.pallas_call(
        _combine_kernel,
        grid=(T // tc,),
        in_specs=[
            pl.BlockSpec(memory_space=pl.ANY),
            pl.BlockSpec(memory_space=pl.ANY),
            pl.BlockSpec((tc, D), lambda i: (i, 0)),
            pl.BlockSpec((tc, 128), lambda i: (i, 0)),
        ],
        out_specs=pl.BlockSpec((tc, D), lambda i: (i, 0)),
        out_shape=jax.ShapeDtypeStruct((T, D), F32),
        scratch_shapes=[
            pltpu.SMEM((2, TOP_K * tc), jnp.int32),
            pltpu.VMEM((2, TOP_K * tc // 8, 8, D), F32),
            pltpu.SemaphoreType.DMA,
            pltpu.SemaphoreType.DMA((2,)),
        ],
        compiler_params=_cparams(1),
    )(pos_tiles, ys, x1, tg)


def _route(top_idx, n_experts, blk):
    T = top_idx.shape[0]
    N = T * TOP_K
    e_flat = top_idx.reshape(N)
    tok_flat = jnp.repeat(jnp.arange(T, dtype=jnp.int32), TOP_K)
    onehot = (e_flat[:, None] == jnp.arange(n_experts, dtype=jnp.int32)[None, :]).astype(jnp.int32)
    csum = jnp.cumsum(onehot, axis=0)
    rank = jnp.sum(csum * onehot, axis=1) - 1
    counts = csum[-1]
    padded = ((counts + blk - 1) // blk) * blk
    pend = jnp.cumsum(padded)
    pstart = pend - padded
    dest = pstart[e_flat] + rank
    n_blocks = -(-(N + n_experts * (blk - 1)) // blk)
    rows = n_blocks * blk
    row_tok = jnp.zeros((rows,), jnp.int32).at[dest].set(tok_flat)
    starts = jnp.arange(n_blocks, dtype=jnp.int32) * blk
    block_e = jnp.minimum(jnp.sum((pend[None, :] <= starts[:, None]).astype(jnp.int32), axis=1),
                          n_experts - 1).astype(jnp.int32)
    n_used = (pend[-1:] // blk).astype(jnp.int32)
    return dest, row_tok.reshape(n_blocks, blk), block_e, n_used


def _cmp_to_sel(nc, n_sel):
    ratio = SEL_BLOCK // CMP_STRIDE
    offs = (np.arange(ratio)[:, None] - np.arange(CMP_LEN // CMP_STRIDE)[None, :]).reshape(-1)
    diff = np.arange(nc)[:, None] - ratio * np.arange(n_sel)[None, :]
    return (diff[..., None] == offs).sum(-1).astype(np.float32)


def _pick(n, pref):
    for c in pref:
        if n % c == 0:
            return c
    return n


def kernel(x, mem, positions, mix_norm_g, mem_norm_g, w_in, w_out, gmlp_ws, gmlp_bs, gmlp_vg,
           nsa_cmp_pos, nsa_ck1, nsa_ck2, nsa_cv1, nsa_cv2, nsa_q_g, nsa_k_g,
           xattn_wkv, xattn_q_g, xattn_k_g, ffn_norm_g, w_router, b_router,
           w_e1, b_e1, w_e2, b_e2):
    B, S, D = x.shape
    T = B * S
    depth = w_in.shape[0]
    n_experts = w_router.shape[-1]
    NC = S // CMP_STRIDE
    n_sel = S // SEL_BLOCK
    tm = _pick(S, (512, 256, 128))
    tk = _pick(S, (512, 256, 128))

    half = HEAD_DIM // 2
    inv = ROPE_THETA ** (-jnp.arange(half, dtype=F32) * 2.0 / HEAD_DIM)
    ang = (positions.astype(F32).reshape(T, 1) * jnp.concatenate([inv, inv])[None, :])
    c2st = jnp.asarray(_cmp_to_sel(NC, n_sel).T, BF16)

    xs = x.reshape(T, D)
    for l in range(depth):
        wi = w_in[l]
        n_gate = 3 * NSA_REP
        zpad = jnp.zeros((D, 128 - n_gate), wi.dtype)
        w_r = jnp.concatenate(
            [wi[:, :3584], wi[:, 3608:4120],
             wi[:, 3584:3584 + n_gate], zpad, wi[:, 3584 + n_gate:3608], zpad], axis=1).astype(BF16)

        proj = _in_proj(xs, mix_norm_g[l][None, :], w_r, tm, PROJ_COLS // 2)

        bias_full = jnp.repeat(jnp.transpose(gmlp_bs[l]), HEAD_DIM, axis=1)
        y_a = _gmlp(proj, gmlp_ws[l], bias_full, gmlp_vg[l].reshape(1, GMLP_WIDTH), T, tm)

        qn, qr, ks, vs, kw, vw = _prep(proj, ang, nsa_q_g[l][None, :], nsa_k_g[l][None, :], B, S, tm)

        w1 = jnp.stack([nsa_ck1[l], nsa_cv1[l]]).astype(BF16)
        w2 = jnp.stack([nsa_ck2[l], nsa_cv2[l]]).astype(BF16)
        cmp_kv = _compress(proj.reshape(B, NC, CMP_STRIDE, PROJ_COLS), w1, w2,
                           nsa_cmp_pos[l], nsa_k_g[l][None, :], B, NC)

        o_c, sel = _cmp_attn(qn, cmp_kv, proj, c2st, B, S, _pick(S, (256, 128)))
        o_s = _sel_attn(qr, ks, vs, proj, sel, B, S, _pick(S, (256, 128)), _pick(S, (1024, 512, 256)),
                        _pick(S, (512, 256)))
        o_w = _win_attn(qr, kw, vw, proj, B, S, _pick(S, (256, 128)))

        km, vm = _mem_kv(mem, mem_norm_g[l][None, :], xattn_wkv[l].astype(BF16), xattn_k_g[l][None, :])
        y_c = _xattn(proj, km, vm, xattn_q_g[l][None, :], B, S, tm)

        wr = jnp.pad(w_router[l], ((0, 0), (0, 128 - n_experts)))
        wr_hi = wr.astype(BF16)
        wr_lo = (wr - wr_hi.astype(F32)).astype(BF16)
        br = jnp.pad(b_router[l], (0, 128 - n_experts))[None, :]
        x1, h2, ti, tg = _out_proj(xs, y_a, o_c, o_s, o_w, y_c, w_out[l].astype(BF16),
                                   ffn_norm_g[l][None, :], wr_hi, wr_lo, br, n_experts,
                                   _pick(S, (256, 128)))

        top_idx = ti[:, :TOP_K].astype(jnp.int32)
        dest, row_tok2, block_e, n_used = _route(top_idx, n_experts, MOE_BLOCK)

        d_ff = w_e2.shape[2]
        w1g, w1l = _deinterleave(w_e1[l].reshape(n_experts * D, 2 * d_ff), 512, _pick(2 * d_ff, (2048, 1024, 512, 256)))
        w1g = w1g.reshape(n_experts, D, d_ff)
        w1l = w1l.reshape(n_experts, D, d_ff)
        b1g = b_e1[l][:, None, 0::2]
        b1l = b_e1[l][:, None, 1::2]
        ys = _moe_experts(block_e, n_used, row_tok2, h2, w1g, w1l, b1g, b1l,
                          w_e2[l].astype(BF16), b_e2[l][:, None, :], _pick(d_ff, (1024, 512)),
                          _pick(D, (512, 256)))

        tc = 128
        pos_tiles = dest.reshape(T // tc, tc, TOP_K).transpose(0, 2, 1).reshape(T // tc, TOP_K * tc)
        xs = _combine(pos_tiles, ys, x1, tg, tc)
    return xs.reshape(B, S, D)
```
